```python
import math
import jax, jax.numpy as jnp
from jax import lax
import numpy as np

D_MODEL = 1024
BATCH = 8
SEQ = 2048
DEPTH = 2

HEAD_DIM = 64
CONV_CH = D_MODEL // 4
LRU_WIDTH = D_MODEL // 4
LRU_HEADS = LRU_WIDTH // HEAD_DIM
ATT_WIDTH = D_MODEL // 2
ATT_HEADS = ATT_WIDTH // HEAD_DIM
D_MIX = CONV_CH + LRU_WIDTH + ATT_WIDTH
D_IN_PROJ = 2 * CONV_CH + 2 * LRU_WIDTH + 3 * ATT_WIDTH
IN_SPLITS = (CONV_CH, 2 * CONV_CH, 2 * CONV_CH + LRU_WIDTH, 2 * CONV_CH + 2 * LRU_WIDTH,
             2 * CONV_CH + 2 * LRU_WIDTH + ATT_WIDTH, 2 * CONV_CH + 2 * LRU_WIDTH + 2 * ATT_WIDTH)
CONV_KERNEL = 31
LRU_CONV_KERNEL = 4
LRU_C = 8.0
DILATED_PATTERNS = ((128, 1), (512, 4), (2048, 16))
ROPE_THETA = 10000.0
N_MEM = 256
MEM_HEADS = 4
MEM_HEAD_DIM = D_MODEL // MEM_HEADS
D_FF = 2816
FFN_CONV_KERNEL = 3
EPS = 1e-6

kernel_name = "hymba_style_conv_lru_dilated_attn_trunk"


def rms_norm(x, g):
    xf = x.astype(jnp.float32)
    y = xf * lax.rsqrt(jnp.mean(jnp.square(xf), -1, keepdims=True) + EPS)
    return (y * g.astype(jnp.float32)).astype(x.dtype)


def layer_norm(x, g, b):
    xf = x.astype(jnp.float32)
    mu = jnp.mean(xf, -1, keepdims=True)
    var = jnp.mean(jnp.square(xf - mu), -1, keepdims=True)
    y = (xf - mu) * lax.rsqrt(var + EPS) * g.astype(jnp.float32) + b.astype(jnp.float32)
    return y.astype(x.dtype)


def causal_dwconv(x, w, b):
    k = w.shape[0]
    y = lax.conv_general_dilated(x, w[:, None, :], window_strides=(1,), padding=((k - 1, 0),),
                                 dimension_numbers=('NWC', 'WIO', 'NWC'),
                                 feature_group_count=x.shape[-1])
    return y + b


def rope_tables(seq, dim):
    inv = 1.0 / (ROPE_THETA ** (jnp.arange(0, dim, 2, dtype=jnp.float32) / dim))
    ang = jnp.arange(seq, dtype=jnp.float32)[:, None] * inv[None, :]
    return jnp.cos(ang), jnp.sin(ang)


def apply_rope(x, cos, sin):
    x1, x2 = jnp.split(x.astype(jnp.float32), 2, axis=-1)
    c = cos[None, :, None, :]
    s = sin[None, :, None, :]
    return jnp.concatenate([x1 * c - x2 * s, x2 * c + x1 * s], axis=-1).astype(x.dtype)


def conformer_conv_mixer(a_val, a_gate, w, b, ln_g, ln_b):
    u = a_val * jax.nn.sigmoid(a_gate)
    u = causal_dwconv(u, w, b)
    u = layer_norm(u, ln_g, ln_b)
    return jax.nn.silu(u)


def rg_lru(x, w_a, b_a, w_i, b_i, lam):
    B, S, C = x.shape
    xh = x.reshape(B, S, LRU_HEADS, C // LRU_HEADS)
    r = jax.nn.sigmoid(jnp.einsum('bshi,hij->bshj', xh, w_a).reshape(B, S, C) + b_a)
    i = jax.nn.sigmoid(jnp.einsum('bshi,hij->bshj', xh, w_i).reshape(B, S, C) + b_i)
    log_a = -LRU_C * r.astype(jnp.float32) * jax.nn.softplus(-lam.astype(jnp.float32))
    a = jnp.exp(log_a)
    bterm = jnp.sqrt(-jnp.expm1(2.0 * log_a)) * (i * x).astype(jnp.float32)

    def combine(e1, e2):
        a1, b1 = e1
        a2, b2 = e2
        return a1 * a2, a2 * b1 + b2

    _, h = lax.associative_scan(combine, (a, bterm), axis=1)
    return h.astype(x.dtype)


def griffin_recurrent_mixer(b_x, b_gate, cw, cb, w_a, b_a, w_i, b_i, lam):
    u = causal_dwconv(b_x, cw, cb)
    h = rg_lru(u, w_a, b_a, w_i, b_i, lam)
    return h * jax.nn.gelu(b_gate)


def dilated_window_attn(q, k, v, window, dilation):
    B, S, H, Dh = q.shape
    n = window // dilation
    L = -(-S // dilation)
    Lp = -(-L // n) * n
    nb = Lp // n

    def to_blocks(t):
        t = jnp.pad(t, ((0, 0), (0, L * dilation - S), (0, 0), (0, 0)))
        t = t.reshape(B, L, dilation, H, Dh)
        t = jnp.pad(t, ((0, 0), (0, Lp - L), (0, 0), (0, 0), (0, 0)))
        return t.reshape(B, nb, n, dilation, H, Dh)

    def with_prev(t):
        prev = jnp.pad(t, ((0, 0), (1, 0), (0, 0), (0, 0), (0, 0), (0, 0)))[:, :-1]
        return jnp.concatenate([prev, t], axis=2)

    qb = to_blocks(q)
    kk = with_prev(to_blocks(k))
    vv = with_prev(to_blocks(v))
    s = jnp.einsum('bnqrhd,bnkrhd->bnrhqk', qb, kk).astype(jnp.float32) * (Dh ** -0.5)
    qi = jnp.arange(n)[:, None]
    ki = jnp.arange(2 * n)[None, :]
    rel = qi - ki + n
    blk_start = jnp.arange(nb)[:, None, None] * n
    mask = (rel >= 0) & (rel <= n) & (blk_start + ki - n >= 0)
    s = jnp.where(mask[None, :, None, None], s, -jnp.inf)
    m = jnp.max(s, axis=-1, keepdims=True)
    p = jnp.exp(s - m)
    den = jnp.sum(p, axis=-1, keepdims=True)
    o = jnp.einsum('bnrhqk,bnkrhd->bnqrhd', (p / den).astype(v.dtype), vv)
    lse = (m + jnp.log(den))[..., 0]
    o = o.reshape(B, Lp, dilation, H, Dh)[:, :L].reshape(B, L * dilation, H, Dh)[:, :S]
    lse = jnp.transpose(lse, (0, 1, 4, 2, 3)).reshape(B, Lp, dilation, H)[:, :L]
    lse = lse.reshape(B, L * dilation, H)[:, :S]
    return o.astype(jnp.float32), lse


def dilated_attention_mixer(q, k, v, cos, sin):
    B, S, _ = q.shape
    q = apply_rope(q.reshape(B, S, ATT_HEADS, HEAD_DIM), cos, sin)
    k = apply_rope(k.reshape(B, S, ATT_HEADS, HEAD_DIM), cos, sin)
    v = v.reshape(B, S, ATT_HEADS, HEAD_DIM)
    outs, lses = zip(*[dilated_window_attn(q, k, v, w, d) for (w, d) in DILATED_PATTERNS])
    wts = jax.nn.softmax(jnp.stack(lses, 0), axis=0)
    o = jnp.sum(wts[..., None] * jnp.stack(outs, 0), axis=0)
    return o.reshape(B, S, ATT_WIDTH).astype(q.dtype)


def memory_cross_attn(h, mem_n, w_q, w_kv, w_o):
    B, S, _ = h.shape
    M = mem_n.shape[1]
    q = (h @ w_q).reshape(B, S, MEM_HEADS, MEM_HEAD_DIM)
    k, v = jnp.split(mem_n @ w_kv, 2, axis=-1)
    k = k.reshape(B, M, MEM_HEADS, MEM_HEAD_DIM)
    v = v.reshape(B, M, MEM_HEADS, MEM_HEAD_DIM)
    s = jnp.einsum('bshd,bmhd->bhsm', q, k).astype(jnp.float32) * (MEM_HEAD_DIM ** -0.5)
    p = jax.nn.softmax(s, axis=-1).astype(v.dtype)
    o = jnp.einsum('bhsm,bmhd->bshd', p, v).reshape(B, S, D_MODEL)
    return o @ w_o


def setup_inputs(seed: int = 0) -> dict:
    key = jax.random.key(seed)
    ks = iter(jax.random.split(key, 40))
    L = DEPTH
    f32 = jnp.float32

    def nrm(shape, fan_in):
        return jax.random.normal(next(ks), shape, f32) * (fan_in ** -0.5)

    def gain(shape):
        return 1.0 + 0.05 * jax.random.normal(next(ks), shape, f32)

    def bias(shape):
        return 0.02 * jax.random.normal(next(ks), shape, f32)

    x = jax.random.normal(next(ks), (BATCH, SEQ, D_MODEL), f32)
    mem = jax.random.normal(next(ks), (BATCH, N_MEM, D_MODEL), f32)
    u = jax.random.uniform(next(ks), (L, LRU_WIDTH), f32, 0.9, 0.999)
    a0 = u ** (1.0 / LRU_C)
    lru_lambda = jnp.log(a0) - jnp.log1p(-a0)
    blk = LRU_WIDTH // LRU_HEADS
    return {
        "x": x,
        "mem": mem,
        "g_mix_pre": gain((L, D_MODEL)),
        "w_in": nrm((L, D_MODEL, D_IN_PROJ), D_MODEL),
        "conv_w": nrm((L, CONV_KERNEL, CONV_CH), CONV_KERNEL),
        "conv_b": bias((L, CONV_CH)),
        "conv_ln_g": gain((L, CONV_CH)),
        "conv_ln_b": bias((L, CONV_CH)),
        "lru_conv_w": nrm((L, LRU_CONV_KERNEL, LRU_WIDTH), LRU_CONV_KERNEL),
        "lru_conv_b": bias((L, LRU_WIDTH)),
        "lru_w_a": nrm((L, LRU_HEADS, blk, blk), blk),
        "lru_b_a": bias((L, LRU_WIDTH)),
        "lru_w_i": nrm((L, LRU_HEADS, blk, blk), blk),
        "lru_b_i": bias((L, LRU_WIDTH)),
        "lru_lambda": lru_lambda,
        "w_out": nrm((L, D_MIX, D_MODEL), D_MIX),
        "g_mix_post": gain((L, D_MODEL)),
        "g_mem_pre": gain((L, D_MODEL)),
        "g_mem_kv": gain((L, D_MODEL)),
        "w_mem_q": nrm((L, D_MODEL, D_MODEL), D_MODEL),
        "w_mem_kv": nrm((L, D_MODEL, 2 * D_MODEL), D_MODEL),
        "w_mem_o": nrm((L, D_MODEL, D_MODEL), D_MODEL),
        "g_mem_post": gain((L, D_MODEL)),
        "g_ffn_pre": gain((L, D_MODEL)),
        "w_up": nrm((L, D_MODEL, 2 * D_FF), D_MODEL),
        "ffn_conv_w": nrm((L, FFN_CONV_KERNEL, 2 * D_FF), FFN_CONV_KERNEL),
        "ffn_conv_b": bias((L, 2 * D_FF)),
        "w_down": nrm((L, D_FF, D_MODEL), D_FF),
        "g_ffn_post": gain((L, D_MODEL)),
    }


def reference(x, mem, g_mix_pre, w_in, conv_w, conv_b, conv_ln_g, conv_ln_b,
              lru_conv_w, lru_conv_b, lru_w_a, lru_b_a, lru_w_i, lru_b_i, lru_lambda,
              w_out, g_mix_post, g_mem_pre, g_mem_kv, w_mem_q, w_mem_kv, w_mem_o, g_mem_post,
              g_ffn_pre, w_up, ffn_conv_w, ffn_conv_b, w_down, g_ffn_post):
    cos, sin = rope_tables(x.shape[1], HEAD_DIM)
    for l in range(DEPTH):
        h = rms_norm(x, g_mix_pre[l])
        a_val, a_gate, b_x, b_gate, q, k, v = jnp.split(h @ w_in[l], IN_SPLITS, axis=-1)
        y_a = conformer_conv_mixer(a_val, a_gate, conv_w[l], conv_b[l], conv_ln_g[l], conv_ln_b[l])
        y_b = griffin_recurrent_mixer(b_x, b_gate, lru_conv_w[l], lru_conv_b[l], lru_w_a[l],
                                      lru_b_a[l], lru_w_i[l], lru_b_i[l], lru_lambda[l])
        y_c = dilated_attention_mixer(q, k, v, cos, sin)
        y = jnp.concatenate([y_a, y_b, y_c], axis=-1) @ w_out[l]
        x = x + rms_norm(y, g_mix_post[l])
        h = rms_norm(x, g_mem_pre[l])
        mem_n = rms_norm(mem, g_mem_kv[l])
        y = memory_cross_attn(h, mem_n, w_mem_q[l], w_mem_kv[l], w_mem_o[l])
        x = x + rms_norm(y, g_mem_post[l])
        h = rms_norm(x, g_ffn_pre[l])
        u = causal_dwconv(h @ w_up[l], ffn_conv_w[l], ffn_conv_b[l])
        gt, up = jnp.split(u, 2, axis=-1)
        y = (jax.nn.gelu(gt) * up) @ w_down[l]
        x = x + rms_norm(y, g_ffn_post[l])
    return x
```

```python
import functools
import math

import numpy as np
import jax
import jax.numpy as jnp
from jax import lax
from jax.experimental import pallas as pl
from jax.experimental.pallas import tpu as pltpu

F32 = jnp.float32
BF16 = jnp.bfloat16

HEAD_DIM = 64
CONV_KERNEL = 31
LRU_CONV_KERNEL = 4
LRU_C = 8.0
DILATED_PATTERNS = ((128, 1), (512, 4), (2048, 16))
ROPE_THETA = 10000.0
MEM_HEADS = 4
FFN_CONV_KERNEL = 3
EPS = 1e-6

LANES = 128
SUBLANES = 8
MXU_DIM = 256
VMEM_LIMIT_BYTES = 56 * 1024 * 1024

ROW_TILE = 512
ATT_Q_TILE = 256
CONV_TILE = 256
CONV_HALO = 32
CONV_SUB = 64
SCAN_CHUNKS = SUBLANES
NEG_BIAS = -1e30


def _params(*sem):
    return pltpu.CompilerParams(dimension_semantics=sem, vmem_limit_bytes=VMEM_LIMIT_BYTES)


def _rms(x, g):
    return x * lax.rsqrt(jnp.mean(x * x, axis=-1, keepdims=True) + EPS) * g


def _full(shape):
    n = len(shape)
    return pl.BlockSpec(shape, lambda *_: (0,) * n)


def _rope(t, cos, sin_signed, first_half):
    n = t.shape[-1]
    half = HEAD_DIM // 2
    partner = jnp.where(first_half, pltpu.roll(t, n - half, axis=1), pltpu.roll(t, half, axis=1))
    return t * cos + partner * sin_signed


def _in_proj_body(x_ref, g_ref, w_ref, cos_ref, sin_ref, a_ref, bx_ref, bg_ref, q_ref, k_ref, v_ref,
                  *, splits):
    hb = _rms(x_ref[...], g_ref[...]).astype(BF16)

    def proj(lo, hi):
        return jnp.dot(hb, w_ref[:, lo:hi], preferred_element_type=F32)

    c_a, c_bx, c_bg, c_q, c_k, c_v = splits
    a_ref[...] = proj(0, c_a)
    bx_ref[...] = proj(c_a, c_bx)
    bg_ref[...] = proj(c_bx, c_bg)
    cos = cos_ref[...]
    sin = sin_ref[...]
    lane = lax.broadcasted_iota(jnp.int32, cos.shape, 1)
    first_half = (lane % HEAD_DIM) < (HEAD_DIM // 2)
    q = _rope(proj(c_bg, c_q), cos, sin, first_half)
    q_ref[...] = (q * (HEAD_DIM ** -0.5)).astype(BF16)
    k_ref[...] = _rope(proj(c_q, c_k), cos, sin, first_half).astype(BF16)
    v_ref[...] = proj(c_k, c_v).astype(BF16)


def _in_proj(x, g, w, cos, sin, conv_ch, lru_w, att_w):
    B, S, D = x.shape
    tm = ROW_TILE
    splits = (2 * conv_ch, 2 * conv_ch + lru_w, 2 * conv_ch + 2 * lru_w,
              2 * conv_ch + 2 * lru_w + att_w, 2 * conv_ch + 2 * lru_w + 2 * att_w,
              2 * conv_ch + 2 * lru_w + 3 * att_w)
    row = lambda n: pl.BlockSpec((None, tm, n), lambda s, b: (b, s, 0))
    tab = pl.BlockSpec((tm, att_w), lambda s, b: (s, 0))
    return pl.pallas_call(
        functools.partial(_in_proj_body, splits=splits),
        grid=(S // tm, B),
        in_specs=[row(D), _full((1, D)), _full(w.shape), tab, tab],
        out_specs=[row(2 * conv_ch), row(lru_w), row(lru_w), row(att_w), row(att_w), row(att_w)],
        out_shape=[jax.ShapeDtypeStruct((B, S, 2 * conv_ch), F32),
                   jax.ShapeDtypeStruct((B, S, lru_w), F32),
                   jax.ShapeDtypeStruct((B, S, lru_w), F32),
                   jax.ShapeDtypeStruct((B, S, att_w), BF16),
                   jax.ShapeDtypeStruct((B, S, att_w), BF16),
                   jax.ShapeDtypeStruct((B, S, att_w), BF16)],
        compiler_params=_params("arbitrary", "arbitrary"),
        name="in_proj",
    )(x, g, w, cos, sin)


def _conv_mixer_body(cur_ref, prev_ref, w_ref, b_ref, g_ref, beta_ref, o_ref, ubuf, *, ch):
    tc = cur_ref.shape[0]
    cur = cur_ref[...]
    ubuf[CONV_HALO:CONV_HALO + tc, :] = cur[:, :ch] * jax.nn.sigmoid(cur[:, ch:])
    prev = prev_ref[tc - CONV_HALO:tc, :]
    halo = prev[:, :ch] * jax.nn.sigmoid(prev[:, ch:])
    ubuf[0:CONV_HALO, :] = jnp.where(pl.program_id(1) > 0, halo, 0.0)
    w = w_ref[...]
    off = CONV_HALO - (CONV_KERNEL - 1)
    for r0 in range(0, tc, CONV_SUB):
        acc = jnp.zeros((CONV_SUB, ch), F32) + b_ref[...]
        for j in range(CONV_KERNEL):
            acc = acc + w[j:j + 1, :] * ubuf[r0 + off + j:r0 + off + j + CONV_SUB, :]
        mu = jnp.mean(acc, axis=-1, keepdims=True)
        cen = acc - mu
        var = jnp.mean(cen * cen, axis=-1, keepdims=True)
        y = cen * lax.rsqrt(var + EPS) * g_ref[...] + beta_ref[...]
        o_ref[r0:r0 + CONV_SUB, :] = (y * jax.nn.sigmoid(y)).astype(o_ref.dtype)


def _conv_mixer(a, w, b, ln_g, ln_b):
    B, S, two_ch = a.shape
    ch = two_ch // 2
    tc = CONV_TILE
    return pl.pallas_call(
        functools.partial(_conv_mixer_body, ch=ch),
        grid=(B, S // tc),
        in_specs=[pl.BlockSpec((None, tc, two_ch), lambda b, i: (b, i, 0)),
                  pl.BlockSpec((None, tc, two_ch), lambda b, i: (b, jnp.maximum(i - 1, 0), 0)),
                  _full(w.shape), _full((1, ch)), _full((1, ch)), _full((1, ch))],
        out_specs=pl.BlockSpec((None, tc, ch), lambda b, i: (b, i, 0)),
        out_shape=jax.ShapeDtypeStruct((B, S, ch), BF16),
        scratch_shapes=[pltpu.VMEM((CONV_HALO + tc, ch), F32)],
        compiler_params=_params("arbitrary", "arbitrary"),
        name="conv_mixer",
    )(a, a, w, b, ln_g, ln_b)


def _lru_body(bx_ref, bg_ref, cw_ref, cb_ref, wa_ref, ba_ref, wi_ref, bi_ref, lam_ref, o_ref,
              xpad, a_s, b_s, *, pitch):
    S, C = bx_ref.shape
    n_slab = C // LANES
    chunk = S // SCAN_CHUNKS
    pad = SUBLANES
    xpad[0:pad, :] = jnp.zeros((pad, C), F32)
    xpad[pad:pad + S, :] = bx_ref[...]
    z = -lam_ref[...]
    softplus = jnp.maximum(z, 0.0) + jnp.log1p(jnp.exp(-jnp.abs(z)))
    cw = cw_ref[...]
    for j in range(SCAN_CHUNKS):
        base = j * chunk
        u = jnp.zeros((chunk, C), F32) + cb_ref[...]
        for t in range(LRU_CONV_KERNEL):
            lo = base + pad - (LRU_CONV_KERNEL - 1) + t
            u = u + cw[t:t + 1, :] * xpad[lo:lo + chunk, :]
        ub = u.astype(BF16)
        r = jax.nn.sigmoid(jnp.dot(ub, wa_ref[...], preferred_element_type=F32) + ba_ref[...])
        gate_i = jax.nn.sigmoid(jnp.dot(ub, wi_ref[...], preferred_element_type=F32) + bi_ref[...])
        log_a = (-LRU_C) * r * softplus
        a = jnp.exp(log_a)
        bterm = jnp.sqrt(jnp.tanh(-log_a) * (1.0 + a * a)) * (gate_i * u)
        for s in range(n_slab):
            a_s[s, j * pitch:j * pitch + chunk, :] = a[:, s * LANES:(s + 1) * LANES]
            b_s[s, j * pitch:j * pitch + chunk, :] = bterm[:, s * LANES:(s + 1) * LANES]

    def step(i, carry):
        out = []
        for s in range(n_slab):
            h, p = carry[2 * s], carry[2 * s + 1]
            idx = pl.ds(i, SCAN_CHUNKS, stride=pitch)
            av = a_s[s, idx, :]
            h = av * h + b_s[s, idx, :]
            p = av * p
            b_s[s, idx, :] = h
            a_s[s, idx, :] = p
            out += [h, p]
        return tuple(out)

    init = []
    for s in range(n_slab):
        init += [jnp.zeros((SCAN_CHUNKS, LANES), F32), jnp.ones((SCAN_CHUNKS, LANES), F32)]
    fin = lax.fori_loop(0, chunk, step, tuple(init), unroll=8)

    for s in range(n_slab):
        h_end, p_end = fin[2 * s], fin[2 * s + 1]
        carry_in = jnp.zeros((1, LANES), F32)
        for j in range(SCAN_CHUNKS):
            rows = slice(j * pitch, j * pitch + chunk)
            h = b_s[s, rows, :] + a_s[s, rows, :] * carry_in
            gate = jax.nn.gelu(bg_ref[j * chunk:(j + 1) * chunk, s * LANES:(s + 1) * LANES])
            o_ref[j * chunk:(j + 1) * chunk, s * LANES:(s + 1) * LANES] = (h * gate).astype(o_ref.dtype)
            carry_in = h_end[j:j + 1, :] + p_end[j:j + 1, :] * carry_in


def _lru_mixer(bx, bg, cw, cb, wa, ba, wi, bi, lam):
    B, S, C = bx.shape
    chunk = S // SCAN_CHUNKS
    pitch = chunk + SUBLANES
    seq = pl.BlockSpec((None, S, C), lambda b: (b, 0, 0))
    return pl.pallas_call(
        functools.partial(_lru_body, pitch=pitch),
        grid=(B,),
        in_specs=[seq, seq, _full(cw.shape), _full((1, C)), _full((C, C)), _full((1, C)),
                  _full((C, C)), _full((1, C)), _full((1, C))],
        out_specs=seq,
        out_shape=jax.ShapeDtypeStruct((B, S, C), BF16),
        scratch_shapes=[pltpu.VMEM((S + SUBLANES, C), F32),
                        pltpu.VMEM((C // LANES, SCAN_CHUNKS * pitch, LANES), F32),
                        pltpu.VMEM((C // LANES, SCAN_CHUNKS * pitch, LANES), F32)],
        compiler_params=_params("arbitrary"),
        name="lru_mixer",
    )(bx, bg, cw, cb, wa, ba, wi, bi, lam)


def _attn_bias_table(seq, tq):
    nq = seq // tq
    r = np.arange(tq)[:, None]
    c = np.arange(seq)[None, :]
    d = r - c + (nq - 1) * tq
    mult = np.zeros(d.shape, np.int64)
    for window, dil in DILATED_PATTERNS:
        n = window // dil
        length = -(-seq // dil)
        assert seq % dil == 0 and length % n == 0, "pattern padding not supported"
        mult += ((d >= 0) & (d % dil == 0) & (d <= window)).astype(np.int64)
    bias = np.where(mult > 0, np.log(np.maximum(mult, 1)), NEG_BIAS)
    return jnp.asarray(bias, F32)


def _attn_body(q_ref, k_ref, v_ref, tb_ref, o_ref):
    S = q_ref.shape[0]
    tq = tb_ref.shape[0]
    nq = S // tq
    lane = lax.broadcasted_iota(jnp.int32, (tq, LANES), 1)
    head0 = lane < HEAD_DIM
    for i in range(nq):
        q = q_ref[i * tq:(i + 1) * tq, :]
        L = (i + 1) * tq
        kk = k_ref[0:L, :]
        vv = v_ref[0:L, :]
        bias = tb_ref[:, (nq - 1 - i) * tq:]
        outs = []
        for hh in range(2):
            keep = head0 if hh == 0 else jnp.logical_not(head0)
            qh = jnp.where(keep, q, jnp.zeros_like(q))
            s = lax.dot_general(qh, kk, (((1,), (1,)), ((), ())), preferred_element_type=F32) + bias
            m = jnp.max(s, axis=-1, keepdims=True)
            p = jnp.exp(s - m)
            den = jnp.sum(p, axis=-1, keepdims=True)
            o = jnp.dot(p.astype(BF16), vv, preferred_element_type=F32)
            outs.append(o / den)
        o_ref[i * tq:(i + 1) * tq, :] = jnp.where(head0, outs[0], outs[1]).astype(o_ref.dtype)


def _dilated_attention(q, k, v, bias):
    B, S, W = q.shape
    blk = pl.BlockSpec((None, S, LANES), lambda b, h: (b, 0, h))
    return pl.pallas_call(
        _attn_body,
        grid=(B, W // LANES),
        in_specs=[blk, blk, blk, _full(bias.shape)],
        out_specs=blk,
        out_shape=jax.ShapeDtypeStruct((B, S, W), BF16),
        compiler_params=_params("arbitrary", "arbitrary"),
        name="dilated_attn",
    )(q, k, v, bias)


def _out_proj_body(ya_ref, yb_ref, yc_ref, w_ref, g_ref, x_ref, o_ref):
    y = jnp.concatenate([ya_ref[...], yb_ref[...], yc_ref[...]], axis=-1)
    y = jnp.dot(y, w_ref[...], preferred_element_type=F32)
    o_ref[...] = x_ref[...] + _rms(y, g_ref[...])


def _out_proj(ya, yb, yc, w, g, x):
    T, D = x.shape
    tm = ROW_TILE
    row = lambda n: pl.BlockSpec((tm, n), lambda i: (i, 0))
    return pl.pallas_call(
        _out_proj_body,
        grid=(T // tm,),
        in_specs=[row(ya.shape[1]), row(yb.shape[1]), row(yc.shape[1]), _full(w.shape),
                  _full((1, D)), row(D)],
        out_specs=row(D),
        out_shape=jax.ShapeDtypeStruct((T, D), F32),
        compiler_params=_params("arbitrary"),
        name="out_proj",
    )(ya, yb, yc, w, g, x)


def _norm_matmul_body(x_ref, g_ref, w_ref, o_ref):
    hb = _rms(x_ref[...], g_ref[...]).astype(BF16)
    o_ref[...] = jnp.dot(hb, w_ref[...], preferred_element_type=F32).astype(o_ref.dtype)


def _norm_matmul(x, g, w, out_dtype):
    T, D = x.shape
    N = w.shape[1]
    tm = ROW_TILE
    return pl.pallas_call(
        _norm_matmul_body,
        grid=(T // tm,),
        in_specs=[pl.BlockSpec((tm, D), lambda i: (i, 0)), _full((1, D)), _full(w.shape)],
        out_specs=pl.BlockSpec((tm, N), lambda i: (i, 0)),
        out_shape=jax.ShapeDtypeStruct((T, N), out_dtype),
        compiler_params=_params("arbitrary"),
        name="mem_kv_proj",
    )(x, g, w)


def _mem_attn_body(x_ref, gpre_ref, wq_ref, kv_ref, wo_ref, gpost_ref, o_ref):
    x = x_ref[...]
    D = x.shape[-1]
    hd = D // MEM_HEADS
    hb = _rms(x, gpre_ref[...]).astype(BF16)
    q = jnp.dot(hb, wq_ref[...], preferred_element_type=F32) * (hd ** -0.5)
    q = q.astype(BF16)
    outs = []
    for h in range(MEM_HEADS):
        qh = q[:, h * hd:(h + 1) * hd]
        kh = kv_ref[:, h * hd:(h + 1) * hd]
        vh = kv_ref[:, D + h * hd:D + (h + 1) * hd]
        s = lax.dot_general(qh, kh, (((1,), (1,)), ((), ())), preferred_element_type=F32)
        m = jnp.max(s, axis=-1, keepdims=True)
        p = jnp.exp(s - m)
        den = jnp.sum(p, axis=-1, keepdims=True)
        o = jnp.dot(p.astype(BF16), vh, preferred_element_type=F32) / den
        outs.append(o.astype(BF16))
    o = jnp.concatenate(outs, axis=-1)
    y = jnp.dot(o, wo_ref[...], preferred_element_type=F32)
    o_ref[...] = x + _rms(y, gpost_ref[...])


def _mem_attn(x, gpre, wq, kv, wo, gpost):
    B, S, D = x.shape
    M = kv.shape[1]
    tm = ROW_TILE
    row = pl.BlockSpec((None, tm, D), lambda b, s: (b, s, 0))
    return pl.pallas_call(
        _mem_attn_body,
        grid=(B, S // tm),
        in_specs=[row, _full((1, D)), _full(wq.shape),
                  pl.BlockSpec((None, M, 2 * D), lambda b, s: (b, 0, 0)),
                  _full(wo.shape), _full((1, D))],
        out_specs=row,
        out_shape=jax.ShapeDtypeStruct((B, S, D), F32),
        compiler_params=_params("arbitrary", "arbitrary"),
        name="mem_attn",
    )(x, gpre, wq, kv, wo, gpost)


def _ffn_body(x_ref, gpre_ref, wup_ref, cw_ref, cb_ref, wd_ref, gpost_ref, o_ref,
              carry, ubuf_g, ubuf_u, act, *, d_ff, tf):
    tm = x_ref.shape[0]
    pad = SUBLANES

    @pl.when(pl.program_id(1) == 0)
    def _():
        carry[...] = jnp.zeros(carry.shape, F32)

    x = x_ref[...]
    hb = _rms(x, gpre_ref[...]).astype(BF16)

    def conv_branch(col0, ubuf):
        cols = slice(col0, col0 + tf)
        u = jnp.dot(hb, wup_ref[:, cols], preferred_element_type=F32)
        ubuf[0:pad, :] = carry[:, cols]
        ubuf[pad:pad + tm, :] = u
        carry[:, cols] = u[tm - pad:tm, :]
        w = cw_ref[:, cols]
        y = cb_ref[:, cols] + w[FFN_CONV_KERNEL - 1:FFN_CONV_KERNEL, :] * u
        for t in range(FFN_CONV_KERNEL - 1):
            back = FFN_CONV_KERNEL - 1 - t
            y = y + w[t:t + 1, :] * ubuf[pad - back:pad - back + tm, :]
        return y

    for c in range(d_ff // tf):
        gate = conv_branch(c * tf, ubuf_g)
        up = conv_branch(d_ff + c * tf, ubuf_u)
        act[:, c * tf:(c + 1) * tf] = (jax.nn.gelu(gate) * up).astype(BF16)
    y = jnp.dot(act[...], wd_ref[...], preferred_element_type=F32)
    o_ref[...] = x + _rms(y, gpost_ref[...])


def _ffn(x, gpre, wup, cw, cb, wd, gpost):
    B, S, D = x.shape
    d_ff = wd.shape[0]
    tm = ROW_TILE
    tf = MXU_DIM
    row = pl.BlockSpec((None, tm, D), lambda b, s: (b, s, 0))
    return pl.pallas_call(
        functools.partial(_ffn_body, d_ff=d_ff, tf=tf),
        grid=(B, S // tm),
        in_specs=[row, _full((1, D)), _full(wup.shape), _full(cw.shape), _full((1, 2 * d_ff)),
                  _full(wd.shape), _full((1, D))],
        out_specs=row,
        out_shape=jax.ShapeDtypeStruct((B, S, D), F32),
        scratch_shapes=[pltpu.VMEM((SUBLANES, 2 * d_ff), F32),
                        pltpu.VMEM((SUBLANES + tm, tf), F32),
                        pltpu.VMEM((SUBLANES + tm, tf), F32),
                        pltpu.VMEM((tm, d_ff), BF16)],
        compiler_params=_params("arbitrary", "arbitrary"),
        name="conv_ffn",
    )(x, gpre, wup, cw, cb, wd, gpost)


def _rope_tables(seq, heads):
    inv = 1.0 / (ROPE_THETA ** (jnp.arange(0, HEAD_DIM, 2, dtype=F32) / HEAD_DIM))
    ang = jnp.arange(seq, dtype=F32)[:, None] * inv[None, :]
    cos, sin = jnp.cos(ang), jnp.sin(ang)
    cos_h = jnp.concatenate([cos, cos], axis=-1)
    sin_h = jnp.concatenate([-sin, sin], axis=-1)
    return jnp.tile(cos_h, (1, heads)), jnp.tile(sin_h, (1, heads))


def _block_diag(w):
    h, n, _ = w.shape
    out = jnp.zeros((h * n, h * n), w.dtype)
    for i in range(h):
        out = out.at[i * n:(i + 1) * n, i * n:(i + 1) * n].set(w[i])
    return out


def kernel(x, mem, g_mix_pre, w_in, conv_w, conv_b, conv_ln_g, conv_ln_b, lru_conv_w, lru_conv_b, lru_w_a, lru_b_a, lru_w_i, lru_b_i, lru_lambda, w_out, g_mix_post, g_mem_pre, g_mem_kv, w_mem_q, w_mem_kv, w_mem_o, g_mem_post, g_ffn_pre, w_up, ffn_conv_w, ffn_conv_b, w_down, g_ffn_post):
    B, S, D = x.shape
    depth = w_in.shape[0]
    conv_ch = conv_w.shape[-1]
    lru_w = lru_conv_w.shape[-1]
    att_w = w_out.shape[1] - conv_ch - lru_w
    n_mem = mem.shape[1]
    assert S % ROW_TILE == 0 and S % CONV_TILE == 0 and S % ATT_Q_TILE == 0
    assert att_w % LANES == 0 and LANES == 2 * HEAD_DIM and (B * n_mem) % ROW_TILE == 0

    cos, sin = _rope_tables(S, att_w // HEAD_DIM)
    bias = _attn_bias_table(S, ATT_Q_TILE)
    row2 = lambda p: p.reshape(1, -1)
    mem2 = mem.reshape(B * n_mem, D)

    for l in range(depth):
        a, bx, bg, q, k, v = _in_proj(x, row2(g_mix_pre[l]), w_in[l].astype(BF16), cos, sin,
                                      conv_ch, lru_w, att_w)
        y_a = _conv_mixer(a, conv_w[l], row2(conv_b[l]), row2(conv_ln_g[l]), row2(conv_ln_b[l]))
        y_b = _lru_mixer(bx, bg, lru_conv_w[l], row2(lru_conv_b[l]),
                         _block_diag(lru_w_a[l]).astype(BF16), row2(lru_b_a[l]),
                         _block_diag(lru_w_i[l]).astype(BF16), row2(lru_b_i[l]),
                         row2(lru_lambda[l]))
        y_c = _dilated_attention(q, k, v, bias)
        T = B * S
        x = _out_proj(y_a.reshape(T, -1), y_b.reshape(T, -1), y_c.reshape(T, -1),
                      w_out[l].astype(BF16), row2(g_mix_post[l]), x.reshape(T, D)).reshape(B, S, D)

        kv = _norm_matmul(mem2, row2(g_mem_kv[l]), w_mem_kv[l].astype(BF16), BF16)
        x = _mem_attn(x, row2(g_mem_pre[l]), w_mem_q[l].astype(BF16), kv.reshape(B, n_mem, 2 * D),
                      w_mem_o[l].astype(BF16), row2(g_mem_post[l]))

        x = _ffn(x, row2(g_ffn_pre[l]), w_up[l].astype(BF16), ffn_conv_w[l], row2(ffn_conv_b[l]),
                 w_down[l].astype(BF16), row2(g_ffn_post[l]))
    return x
```

```python
import functools

import numpy as np
import jax
import jax.numpy as jnp
from jax import lax
from jax.experimental import pallas as pl
from jax.experimental.pallas import tpu as pltpu

F32 = jnp.float32
BF16 = jnp.bfloat16

HEAD_DIM = 64
CONV_KERNEL = 31
LRU_CONV_KERNEL = 4
LRU_C = 8.0
DILATED_PATTERNS = ((128, 1), (512, 4), (2048, 16))
ROPE_THETA = 10000.0
MEM_HEADS = 4
FFN_CONV_KERNEL = 3
EPS = 1e-6

LANES = 128
SUBLANES = 8
MXU_DIM = 256
VMEM_LIMIT_BYTES = 56 * 1024 * 1024

ROW_TILE = 512
ATT_Q_TILE = 256
CONV_TILE = 256
CONV_HALO = 32
CONV_SUB = 64
SCAN_CHUNKS = SUBLANES
NEG_BIAS = -1e30


def _params(*sem):
    return pltpu.CompilerParams(dimension_semantics=sem, vmem_limit_bytes=VMEM_LIMIT_BYTES)


def _rms(x, g):
    return x * lax.rsqrt(jnp.mean(x * x, axis=-1, keepdims=True) + EPS) * g


def _full(shape):
    n = len(shape)
    return pl.BlockSpec(shape, lambda *_: (0,) * n)


def _layer(p, l):
    return pl.BlockSpec((None,) + p.shape[1:], lambda *_: (l, 0, 0), pipeline_mode=pl.Buffered(1))


def _rope(t, cos, sin_signed, first_half):
    n = t.shape[-1]
    half = HEAD_DIM // 2
    partner = jnp.where(first_half, pltpu.roll(t, n - half, axis=1), pltpu.roll(t, half, axis=1))
    return t * cos + partner * sin_signed


def _in_proj_body(x_ref, g_ref, w_ref, cos_ref, sin_ref, a_ref, bx_ref, bg_ref, q_ref, k_ref, v_ref,
                  *, splits):
    hb = _rms(x_ref[...], g_ref[...]).astype(BF16)

    def proj(lo, hi):
        return jnp.dot(hb, w_ref[:, lo:hi], preferred_element_type=F32)

    c_a, c_bx, c_bg, c_q, c_k, c_v = splits
    a_ref[...] = proj(0, c_a)
    bx_ref[...] = proj(c_a, c_bx)
    bg_ref[...] = proj(c_bx, c_bg)
    cos = cos_ref[...]
    sin = sin_ref[...]
    lane = lax.broadcasted_iota(jnp.int32, cos.shape, 1)
    first_half = (lane % HEAD_DIM) < (HEAD_DIM // 2)
    q = _rope(proj(c_bg, c_q), cos, sin, first_half)
    q_ref[...] = (q * (HEAD_DIM ** -0.5)).astype(BF16)
    k_ref[...] = _rope(proj(c_q, c_k), cos, sin, first_half).astype(BF16)
    v_ref[...] = proj(c_k, c_v).astype(BF16)


def _in_proj(l, x, g, w, cos, sin, conv_ch, lru_w, att_w):
    B, S, D = x.shape
    tm = ROW_TILE
    splits = (2 * conv_ch, 2 * conv_ch + lru_w, 2 * conv_ch + 2 * lru_w,
              2 * conv_ch + 2 * lru_w + att_w, 2 * conv_ch + 2 * lru_w + 2 * att_w,
              2 * conv_ch + 2 * lru_w + 3 * att_w)
    row = lambda n: pl.BlockSpec((None, tm, n), lambda s, b: (b, s, 0))
    tab = pl.BlockSpec((tm, att_w), lambda s, b: (s, 0))
    return pl.pallas_call(
        functools.partial(_in_proj_body, splits=splits),
        grid=(S // tm, B),
        in_specs=[row(D), _layer(g, l), _layer(w, l), tab, tab],
        out_specs=[row(2 * conv_ch), row(lru_w), row(lru_w), row(att_w), row(att_w), row(att_w)],
        out_shape=[jax.ShapeDtypeStruct((B, S, 2 * conv_ch), F32),
                   jax.ShapeDtypeStruct((B, S, lru_w), F32),
                   jax.ShapeDtypeStruct((B, S, lru_w), F32),
                   jax.ShapeDtypeStruct((B, S, att_w), BF16),
                   jax.ShapeDtypeStruct((B, S, att_w), BF16),
                   jax.ShapeDtypeStruct((B, S, att_w), BF16)],
        compiler_params=_params("arbitrary", "arbitrary"),
        name="in_proj",
    )(x, g, w, cos, sin)


def _conv_mixer_body(cur_ref, prev_ref, w_ref, b_ref, g_ref, beta_ref, o_ref, ubuf, *, ch):
    tc = cur_ref.shape[0]
    cur = cur_ref[...]
    ubuf[CONV_HALO:CONV_HALO + tc, :] = cur[:, :ch] * jax.nn.sigmoid(cur[:, ch:])
    prev = prev_ref[tc - CONV_HALO:tc, :]
    halo = prev[:, :ch] * jax.nn.sigmoid(prev[:, ch:])
    ubuf[0:CONV_HALO, :] = jnp.where(pl.program_id(1) > 0, halo, 0.0)
    w = w_ref[...]
    off = CONV_HALO - (CONV_KERNEL - 1)
    for r0 in range(0, tc, CONV_SUB):
        acc = jnp.zeros((CONV_SUB, ch), F32) + b_ref[...]
        for j in range(CONV_KERNEL):
            acc = acc + w[j:j + 1, :] * ubuf[r0 + off + j:r0 + off + j + CONV_SUB, :]
        mu = jnp.mean(acc, axis=-1, keepdims=True)
        cen = acc - mu
        var = jnp.mean(cen * cen, axis=-1, keepdims=True)
        y = cen * lax.rsqrt(var + EPS) * g_ref[...] + beta_ref[...]
        o_ref[r0:r0 + CONV_SUB, :] = (y * jax.nn.sigmoid(y)).astype(o_ref.dtype)


def _conv_mixer(l, a, w, b, ln_g, ln_b):
    B, S, two_ch = a.shape
    ch = two_ch // 2
    tc = CONV_TILE
    return pl.pallas_call(
        functools.partial(_conv_mixer_body, ch=ch),
        grid=(B, S // tc),
        in_specs=[pl.BlockSpec((None, tc, two_ch), lambda b, i: (b, i, 0)),
                  pl.BlockSpec((None, tc, two_ch), lambda b, i: (b, jnp.maximum(i - 1, 0), 0)),
                  _layer(w, l), _layer(b, l), _layer(ln_g, l), _layer(ln_b, l)],
        out_specs=pl.BlockSpec((None, tc, ch), lambda b, i: (b, i, 0)),
        out_shape=jax.ShapeDtypeStruct((B, S, ch), BF16),
        scratch_shapes=[pltpu.VMEM((CONV_HALO + tc, ch), F32)],
        compiler_params=_params("arbitrary", "arbitrary"),
        name="conv_mixer",
    )(a, a, w, b, ln_g, ln_b)


def _lru_body(bx_ref, bg_ref, cw_ref, cb_ref, wa_ref, ba_ref, wi_ref, bi_ref, lam_ref, o_ref,
              xpad, a_s, b_s, *, pitch):
    S, C = bx_ref.shape
    n_slab = C // LANES
    chunk = S // SCAN_CHUNKS
    pad = SUBLANES
    xpad[0:pad, :] = jnp.zeros((pad, C), F32)
    xpad[pad:pad + S, :] = bx_ref[...]
    z = -lam_ref[...]
    softplus = jnp.maximum(z, 0.0) + jnp.log1p(jnp.exp(-jnp.abs(z)))
    cw = cw_ref[...]
    for j in range(SCAN_CHUNKS):
        base = j * chunk
        u = jnp.zeros((chunk, C), F32) + cb_ref[...]
        for t in range(LRU_CONV_KERNEL):
            lo = base + pad - (LRU_CONV_KERNEL - 1) + t
            u = u + cw[t:t + 1, :] * xpad[lo:lo + chunk, :]
        ub = u.astype(BF16)
        r = jax.nn.sigmoid(jnp.dot(ub, wa_ref[...], preferred_element_type=F32) + ba_ref[...])
        gate_i = jax.nn.sigmoid(jnp.dot(ub, wi_ref[...], preferred_element_type=F32) + bi_ref[...])
        log_a = (-LRU_C) * r * softplus
        a = jnp.exp(log_a)
        bterm = jnp.sqrt(jnp.tanh(-log_a) * (1.0 + a * a)) * (gate_i * u)
        for s in range(n_slab):
            a_s[s, j * pitch:j * pitch + chunk, :] = a[:, s * LANES:(s + 1) * LANES]
            b_s[s, j * pitch:j * pitch + chunk, :] = bterm[:, s * LANES:(s + 1) * LANES]

    def step(i, carry):
        out = []
        for s in range(n_slab):
            h, p = carry[2 * s], carry[2 * s + 1]
            idx = pl.ds(i, SCAN_CHUNKS, stride=pitch)
            av = a_s[s, idx, :]
            h = av * h + b_s[s, idx, :]
            p = av * p
            b_s[s, idx, :] = h
            a_s[s, idx, :] = p
            out += [h, p]
        return tuple(out)

    init = []
    for s in range(n_slab):
        init += [jnp.zeros((SCAN_CHUNKS, LANES), F32), jnp.ones((SCAN_CHUNKS, LANES), F32)]
    fin = lax.fori_loop(0, chunk, step, tuple(init), unroll=8)

    for s in range(n_slab):
        h_end, p_end = fin[2 * s], fin[2 * s + 1]
        carry_in = jnp.zeros((1, LANES), F32)
        for j in range(SCAN_CHUNKS):
            rows = slice(j * pitch, j * pitch + chunk)
            h = b_s[s, rows, :] + a_s[s, rows, :] * carry_in
            gate = jax.nn.gelu(bg_ref[j * chunk:(j + 1) * chunk, s * LANES:(s + 1) * LANES])
            o_ref[j * chunk:(j + 1) * chunk, s * LANES:(s + 1) * LANES] = (h * gate).astype(o_ref.dtype)
            carry_in = h_end[j:j + 1, :] + p_end[j:j + 1, :] * carry_in


def _lru_mixer(l, bx, bg, cw, cb, wa, ba, wi, bi, lam):
    B, S, C = bx.shape
    chunk = S // SCAN_CHUNKS
    pitch = chunk + SUBLANES
    seq = pl.BlockSpec((None, S, C), lambda b: (b, 0, 0))
    params = (cw, cb, wa, ba, wi, bi, lam)
    return pl.pallas_call(
        functools.partial(_lru_body, pitch=pitch),
        grid=(B,),
        in_specs=[seq, seq] + [_layer(p, l) for p in params],
        out_specs=seq,
        out_shape=jax.ShapeDtypeStruct((B, S, C), BF16),
        scratch_shapes=[pltpu.VMEM((S + SUBLANES, C), F32),
                        pltpu.VMEM((C // LANES, SCAN_CHUNKS * pitch, LANES), F32),
                        pltpu.VMEM((C // LANES, SCAN_CHUNKS * pitch, LANES), F32)],
        compiler_params=_params("arbitrary"),
        name="lru_mixer",
    )(bx, bg, *params)


def _attn_bias_table(seq, tq):
    nq = seq // tq
    r = np.arange(tq)[:, None]
    c = np.arange(seq)[None, :]
    d = r - c + (nq - 1) * tq
    mult = np.zeros(d.shape, np.int64)
    for window, dil in DILATED_PATTERNS:
        n = window // dil
        length = -(-seq // dil)
        assert seq % dil == 0 and length % n == 0, "pattern padding not supported"
        mult += ((d >= 0) & (d % dil == 0) & (d <= window)).astype(np.int64)
    bias = np.where(mult > 0, np.log(np.maximum(mult, 1)), NEG_BIAS)
    return jnp.asarray(bias, F32)


def _attn_body(q_ref, k_ref, v_ref, tb_ref, o_ref):
    S = q_ref.shape[0]
    tq = tb_ref.shape[0]
    nq = S // tq
    lane = lax.broadcasted_iota(jnp.int32, (tq, LANES), 1)
    head0 = lane < HEAD_DIM
    for i in range(nq):
        q = q_ref[i * tq:(i + 1) * tq, :]
        L = (i + 1) * tq
        kk = k_ref[0:L, :]
        vv = v_ref[0:L, :]
        bias = tb_ref[:, (nq - 1 - i) * tq:]
        outs = []
        for hh in range(2):
            keep = head0 if hh == 0 else jnp.logical_not(head0)
            qh = jnp.where(keep, q, jnp.zeros_like(q))
            s = lax.dot_general(qh, kk, (((1,), (1,)), ((), ())), preferred_element_type=F32) + bias
            m = jnp.max(s, axis=-1, keepdims=True)
            p = jnp.exp(s - m)
            den = jnp.sum(p, axis=-1, keepdims=True)
            o = jnp.dot(p.astype(BF16), vv, preferred_element_type=F32)
            outs.append(o / den)
        o_ref[i * tq:(i + 1) * tq, :] = jnp.where(head0, outs[0], outs[1]).astype(o_ref.dtype)


def _dilated_attention(q, k, v, bias):
    B, S, W = q.shape
    blk = pl.BlockSpec((None, S, LANES), lambda b, h: (b, 0, h))
    return pl.pallas_call(
        _attn_body,
        grid=(B, W // LANES),
        in_specs=[blk, blk, blk, _full(bias.shape)],
        out_specs=blk,
        out_shape=jax.ShapeDtypeStruct((B, S, W), BF16),
        compiler_params=_params("arbitrary", "arbitrary"),
        name="dilated_attn",
    )(q, k, v, bias)


def _norm_matmul_body(x_ref, g_ref, w_ref, o_ref):
    hb = _rms(x_ref[...], g_ref[...]).astype(BF16)
    o_ref[...] = jnp.dot(hb, w_ref[...], preferred_element_type=F32).astype(o_ref.dtype)


def _mem_kv_proj(l, x, g, w):
    T, D = x.shape
    N = w.shape[-1]
    tm = ROW_TILE
    return pl.pallas_call(
        _norm_matmul_body,
        grid=(T // tm,),
        in_specs=[pl.BlockSpec((tm, D), lambda i: (i, 0)), _layer(g, l), _layer(w, l)],
        out_specs=pl.BlockSpec((tm, N), lambda i: (i, 0)),
        out_shape=jax.ShapeDtypeStruct((T, N), BF16),
        compiler_params=_params("arbitrary"),
        name="mem_kv_proj",
    )(x, g, w)


def _mix_out_mem_body(ya_ref, yb_ref, yc_ref, x_ref, wout_ref, gmix_ref, gpre_ref, wq_ref, kv_ref,
                      wo_ref, gpost_ref, o_ref):
    y = jnp.concatenate([ya_ref[...], yb_ref[...], yc_ref[...]], axis=-1)
    y = jnp.dot(y, wout_ref[...], preferred_element_type=F32)
    x = x_ref[...] + _rms(y, gmix_ref[...])

    D = x.shape[-1]
    hd = D // MEM_HEADS
    hb = _rms(x, gpre_ref[...]).astype(BF16)
    q = jnp.dot(hb, wq_ref[...], preferred_element_type=F32) * (hd ** -0.5)
    q = q.astype(BF16)
    outs = []
    for h in range(MEM_HEADS):
        qh = q[:, h * hd:(h + 1) * hd]
        kh = kv_ref[:, h * hd:(h + 1) * hd]
        vh = kv_ref[:, D + h * hd:D + (h + 1) * hd]
        s = lax.dot_general(qh, kh, (((1,), (1,)), ((), ())), preferred_element_type=F32)
        m = jnp.max(s, axis=-1, keepdims=True)
        p = jnp.exp(s - m)
        den = jnp.sum(p, axis=-1, keepdims=True)
        o = jnp.dot(p.astype(BF16), vh, preferred_element_type=F32) / den
        outs.append(o.astype(BF16))
    o = jnp.concatenate(outs, axis=-1)
    y = jnp.dot(o, wo_ref[...], preferred_element_type=F32)
    o_ref[...] = x + _rms(y, gpost_ref[...])


def _mix_out_mem(l, ya, yb, yc, x, wout, gmix, gpre, wq, kv, wo, gpost):
    B, S, D = x.shape
    M = kv.shape[1]
    tm = ROW_TILE
    row = lambda n: pl.BlockSpec((None, tm, n), lambda b, s: (b, s, 0))
    return pl.pallas_call(
        _mix_out_mem_body,
        grid=(B, S // tm),
        in_specs=[row(ya.shape[-1]), row(yb.shape[-1]), row(yc.shape[-1]), row(D),
                  _layer(wout, l), _layer(gmix, l), _layer(gpre, l), _layer(wq, l),
                  pl.BlockSpec((None, M, 2 * D), lambda b, s: (b, 0, 0)),
                  _layer(wo, l), _layer(gpost, l)],
        out_specs=row(D),
        out_shape=jax.ShapeDtypeStruct((B, S, D), F32),
        compiler_params=_params("arbitrary", "arbitrary"),
        name="mix_out_mem_attn",
    )(ya, yb, yc, x, wout, gmix, gpre, wq, kv, wo, gpost)


def _ffn_body(x_ref, gpre_ref, wup_ref, cw_ref, cb_ref, wd_ref, gpost_ref, o_ref,
              carry, ubuf_g, ubuf_u, act, *, d_ff, tf):
    tm = x_ref.shape[0]
    pad = SUBLANES

    @pl.when(pl.program_id(1) == 0)
    def _():
        carry[...] = jnp.zeros(carry.shape, F32)

    x = x_ref[...]
    hb = _rms(x, gpre_ref[...]).astype(BF16)

    def conv_branch(col0, ubuf):
        cols = slice(col0, col0 + tf)
        u = jnp.dot(hb, wup_ref[:, cols], preferred_element_type=F32)
        ubuf[0:pad, :] = carry[:, cols]
        ubuf[pad:pad + tm, :] = u
        carry[:, cols] = u[tm - pad:tm, :]
        w = cw_ref[:, cols]
        y = cb_ref[:, cols] + w[FFN_CONV_KERNEL - 1:FFN_CONV_KERNEL, :] * u
        for t in range(FFN_CONV_KERNEL - 1):
            back = FFN_CONV_KERNEL - 1 - t
            y = y + w[t:t + 1, :] * ubuf[pad - back:pad - back + tm, :]
        return y

    for c in range(d_ff // tf):
        gate = conv_branch(c * tf, ubuf_g)
        up = conv_branch(d_ff + c * tf, ubuf_u)
        act[:, c * tf:(c + 1) * tf] = (jax.nn.gelu(gate) * up).astype(BF16)
    y = jnp.dot(act[...], wd_ref[...], preferred_element_type=F32)
    o_ref[...] = x + _rms(y, gpost_ref[...])


def _ffn(l, x, gpre, wup, cw, cb, wd, gpost):
    B, S, D = x.shape
    d_ff = wd.shape[1]
    tm = ROW_TILE
    tf = MXU_DIM
    row = pl.BlockSpec((None, tm, D), lambda b, s: (b, s, 0))
    return pl.pallas_call(
        functools.partial(_ffn_body, d_ff=d_ff, tf=tf),
        grid=(B, S // tm),
        in_specs=[row, _layer(gpre, l), _layer(wup, l), _layer(cw, l), _layer(cb, l),
                  _layer(wd, l), _layer(gpost, l)],
        out_specs=row,
        out_shape=jax.ShapeDtypeStruct((B, S, D), F32),
        scratch_shapes=[pltpu.VMEM((SUBLANES, 2 * d_ff), F32),
                        pltpu.VMEM((SUBLANES + tm, tf), F32),
                        pltpu.VMEM((SUBLANES + tm, tf), F32),
                        pltpu.VMEM((tm, d_ff), BF16)],
        compiler_params=_params("arbitrary", "arbitrary"),
        name="conv_ffn",
    )(x, gpre, wup, cw, cb, wd, gpost)


def _rope_tables(seq, heads):
    inv = 1.0 / (ROPE_THETA ** (jnp.arange(0, HEAD_DIM, 2, dtype=F32) / HEAD_DIM))
    ang = jnp.arange(seq, dtype=F32)[:, None] * inv[None, :]
    cos, sin = jnp.cos(ang), jnp.sin(ang)
    cos_h = jnp.concatenate([cos, cos], axis=-1)
    sin_h = jnp.concatenate([-sin, sin], axis=-1)
    return jnp.tile(cos_h, (1, heads)), jnp.tile(sin_h, (1, heads))


def _block_diag(w):
    depth, h, n, _ = w.shape
    eye = jnp.eye(h, dtype=w.dtype)
    return jnp.einsum('lhij,hg->lhigj', w, eye).reshape(depth, h * n, h * n)


def kernel(x, mem, g_mix_pre, w_in, conv_w, conv_b, conv_ln_g, conv_ln_b, lru_conv_w, lru_conv_b, lru_w_a, lru_b_a, lru_w_i, lru_b_i, lru_lambda, w_out, g_mix_post, g_mem_pre, g_mem_kv, w_mem_q, w_mem_kv, w_mem_o, g_mem_post, g_ffn_pre, w_up, ffn_conv_w, ffn_conv_b, w_down, g_ffn_post):
    B, S, D = x.shape
    depth = w_in.shape[0]
    conv_ch = conv_w.shape[-1]
    lru_w = lru_conv_w.shape[-1]
    att_w = w_out.shape[1] - conv_ch - lru_w
    n_mem = mem.shape[1]
    assert S % ROW_TILE == 0 and S % CONV_TILE == 0 and S % ATT_Q_TILE == 0
    assert att_w % LANES == 0 and LANES == 2 * HEAD_DIM and (B * n_mem) % ROW_TILE == 0

    cos, sin = _rope_tables(S, att_w // HEAD_DIM)
    bias = _attn_bias_table(S, ATT_Q_TILE)
    vec = lambda p: p.reshape(depth, 1, -1)
    mxu = lambda p: p.astype(BF16)
    mem2 = mem.reshape(B * n_mem, D)

    g_mix_pre, g_mix_post, g_mem_pre, g_mem_kv, g_mem_post, g_ffn_pre, g_ffn_post = map(
        vec, (g_mix_pre, g_mix_post, g_mem_pre, g_mem_kv, g_mem_post, g_ffn_pre, g_ffn_post))
    conv_b, conv_ln_g, conv_ln_b, lru_conv_b, lru_b_a, lru_b_i, lru_lambda, ffn_conv_b = map(
        vec, (conv_b, conv_ln_g, conv_ln_b, lru_conv_b, lru_b_a, lru_b_i, lru_lambda, ffn_conv_b))
    w_in, w_out, w_mem_q, w_mem_kv, w_mem_o, w_up, w_down = map(
        mxu, (w_in, w_out, w_mem_q, w_mem_kv, w_mem_o, w_up, w_down))
    lru_w_a = mxu(_block_diag(lru_w_a))
    lru_w_i = mxu(_block_diag(lru_w_i))

    for l in range(depth):
        a, bx, bg, q, k, v = _in_proj(l, x, g_mix_pre, w_in, cos, sin, conv_ch, lru_w, att_w)
        y_a = _conv_mixer(l, a, conv_w, conv_b, conv_ln_g, conv_ln_b)
        y_b = _lru_mixer(l, bx, bg, lru_conv_w, lru_conv_b, lru_w_a, lru_b_a, lru_w_i, lru_b_i,
                         lru_lambda)
        y_c = _dilated_attention(q, k, v, bias)
        kv = _mem_kv_proj(l, mem2, g_mem_kv, w_mem_kv).reshape(B, n_mem, 2 * D)
        x = _mix_out_mem(l, y_a, y_b, y_c, x, w_out, g_mix_post, g_mem_pre, w_mem_q, kv, w_mem_o,
                         g_mem_post)
        x = _ffn(l, x, g_ffn_pre, w_up, ffn_conv_w, ffn_conv_b, w_down, g_ffn_post)
    return x
```

```python
import functools

import numpy as np
import jax
import jax.numpy as jnp
from jax import lax
from jax.experimental import pallas as pl
from jax.experimental.pallas import tpu as pltpu

F32 = jnp.float32
BF16 = jnp.bfloat16

HEAD_DIM = 64
CONV_KERNEL = 31
LRU_CONV_KERNEL = 4
LRU_C = 8.0
DILATED_PATTERNS = ((128, 1), (512, 4), (2048, 16))
ROPE_THETA = 10000.0
MEM_HEADS = 4
FFN_CONV_KERNEL = 3
EPS = 1e-6

LANES = 128
SUBLANES = 8
MXU_DIM = 256
VMEM_LIMIT_BYTES = 56 * 1024 * 1024

ROW_TILE = 512
ATT_BLOCK = 128
ATT_PERM = 4
ATT_PIPELINE = 4
CONV_HALO = 32
CONV_SUB = 64
LOG2E = 1.4426950408889634
SCAN_CHUNKS = SUBLANES
NEG_BIAS = -1e30


def _params(*sem):
    return pltpu.CompilerParams(dimension_semantics=sem, vmem_limit_bytes=VMEM_LIMIT_BYTES)


def _rms(x, g):
    return x * lax.rsqrt(jnp.mean(x * x, axis=-1, keepdims=True) + EPS) * g


def _full(shape):
    n = len(shape)
    return pl.BlockSpec(shape, lambda *_: (0,) * n)


def _layer(p, l):
    return pl.BlockSpec((None,) + p.shape[1:], lambda *_: (l, 0, 0), pipeline_mode=pl.Buffered(1))


def _rope(t, cos, sin_signed, first_half):
    n = t.shape[-1]
    half = HEAD_DIM // 2
    partner = jnp.where(first_half, pltpu.roll(t, n - half, axis=1), pltpu.roll(t, half, axis=1))
    return t * cos + partner * sin_signed


def _store_attn_operand(t, nat_ref, perm_ref, stage):
    rows, width = t.shape
    nat_ref[...] = t.astype(BF16)
    for c in range(width // LANES):
        stage[c] = t[:, c * LANES:(c + 1) * LANES]
    for c in range(width // LANES):
        for r in range(ATT_PERM):
            sub = stage[c, pl.ds(r, rows // ATT_PERM, stride=ATT_PERM), :]
            perm_ref[r, :, c * LANES:(c + 1) * LANES] = sub.astype(BF16)


def _conv_mixer_tile(a, w_ref, b_ref, g_ref, beta_ref, o_ref, ubuf, ush, *, ch):
    tm = a.shape[0]
    @pl.when(pl.program_id(1) == 0)
    def _():
        ubuf[0:CONV_HALO, :] = jnp.zeros((CONV_HALO, ch), F32)

    @pl.when(pl.program_id(1) > 0)
    def _():
        ubuf[0:CONV_HALO, :] = ubuf[tm:tm + CONV_HALO, :]

    ubuf[CONV_HALO:CONV_HALO + tm, :] = a[:, :ch] * jax.nn.sigmoid(a[:, ch:])
    span = CONV_HALO + tm - SUBLANES
    for s in range(1, SUBLANES):
        ush[s - 1, 0:span, :] = ubuf[s:s + span, :]
    w = w_ref[...]
    off = CONV_HALO - (CONV_KERNEL - 1)
    for r0 in range(0, tm, CONV_SUB):
        acc = jnp.zeros((CONV_SUB, ch), F32) + b_ref[...]
        for j in range(CONV_KERNEL):
            s = (off + j) % SUBLANES
            lo = r0 + off + j - s
            win = ubuf[lo:lo + CONV_SUB, :] if s == 0 else ush[s - 1, lo:lo + CONV_SUB, :]
            acc = acc + w[j:j + 1, :] * win
        mu = jnp.mean(acc, axis=-1, keepdims=True)
        cen = acc - mu
        var = jnp.mean(cen * cen, axis=-1, keepdims=True)
        y = cen * lax.rsqrt(var + EPS) * g_ref[...] + beta_ref[...]
        o_ref[r0:r0 + CONV_SUB, :] = (y * jax.nn.sigmoid(y)).astype(o_ref.dtype)


def _in_proj_body(x_ref, g_ref, w_ref, cos_ref, sin_ref, cw_ref, cb_ref, cg_ref, cbeta_ref,
                  ya_ref, bx_ref, bg_ref, q_ref, k_ref, v_ref, qp_ref, kp_ref, vp_ref,
                  ubuf, ush, stage_q, stage_k, stage_v, *, splits):
    tm = x_ref.shape[0]
    hb = _rms(x_ref[...], g_ref[...]).astype(BF16)

    def proj(lo, hi):
        return jnp.dot(hb, w_ref[:, lo:hi], preferred_element_type=F32)

    c_a, c_bx, c_bg, c_q, c_k, c_v = splits
    _conv_mixer_tile(proj(0, c_a), cw_ref, cb_ref, cg_ref, cbeta_ref, ya_ref, ubuf, ush, ch=c_a // 2)
    bx_ref[...] = proj(c_a, c_bx)
    bg_ref[...] = proj(c_bx, c_bg)
    rows = pl.ds(pl.multiple_of(pl.program_id(1) * tm, tm), tm)
    cos = cos_ref[rows, :]
    sin = sin_ref[rows, :]
    lane = lax.broadcasted_iota(jnp.int32, cos.shape, 1)
    first_half = (lane % HEAD_DIM) < (HEAD_DIM // 2)
    q = _rope(proj(c_bg, c_q), cos, sin, first_half) * (HEAD_DIM ** -0.5 * LOG2E)
    _store_attn_operand(q, q_ref, qp_ref, stage_q)
    _store_attn_operand(_rope(proj(c_q, c_k), cos, sin, first_half), k_ref, kp_ref, stage_k)
    _store_attn_operand(proj(c_k, c_v), v_ref, vp_ref, stage_v)


def _in_proj(l, x, g, w, cos, sin, conv_w, conv_b, conv_g, conv_beta, conv_ch, lru_w, att_w):
    B, S, D = x.shape
    tm = ROW_TILE
    splits = (2 * conv_ch, 2 * conv_ch + lru_w, 2 * conv_ch + 2 * lru_w,
              2 * conv_ch + 2 * lru_w + att_w, 2 * conv_ch + 2 * lru_w + 2 * att_w,
              2 * conv_ch + 2 * lru_w + 3 * att_w)
    row = lambda n: pl.BlockSpec((None, tm, n), lambda b, s: (b, s, 0))
    perm = pl.BlockSpec((None, ATT_PERM, tm // ATT_PERM, att_w), lambda b, s: (b, 0, s, 0))
    nat_shape = jax.ShapeDtypeStruct((B, S, att_w), BF16)
    perm_shape = jax.ShapeDtypeStruct((B, ATT_PERM, S // ATT_PERM, att_w), BF16)
    stage = pltpu.VMEM((att_w // LANES, tm, LANES), F32)
    return pl.pallas_call(
        functools.partial(_in_proj_body, splits=splits),
        grid=(B, S // tm),
        in_specs=[row(D), _layer(g, l), _layer(w, l), _full(cos.shape), _full(sin.shape),
                  _layer(conv_w, l), _layer(conv_b, l), _layer(conv_g, l), _layer(conv_beta, l)],
        out_specs=[row(conv_ch), row(lru_w), row(lru_w), row(att_w), row(att_w), row(att_w),
                   perm, perm, perm],
        out_shape=[jax.ShapeDtypeStruct((B, S, conv_ch), BF16),
                   jax.ShapeDtypeStruct((B, S, lru_w), F32),
                   jax.ShapeDtypeStruct((B, S, lru_w), F32),
                   nat_shape, nat_shape, nat_shape, perm_shape, perm_shape, perm_shape],
        scratch_shapes=[pltpu.VMEM((CONV_HALO + tm, conv_ch), F32),
                        pltpu.VMEM((SUBLANES - 1, CONV_HALO + tm - SUBLANES, conv_ch), F32),
                        stage, stage, stage],
        compiler_params=_params("arbitrary", "arbitrary"),
        name="in_proj",
    )(x, g, w, cos, sin, conv_w, conv_b, conv_g, conv_beta)


def _lru_body(bx_ref, bg_ref, cw_ref, cb_ref, wa_ref, ba_ref, wi_ref, bi_ref, lam_ref, o_ref,
              xpad, a_s, b_s, *, pitch):
    S, C = bx_ref.shape
    n_slab = C // LANES
    chunk = S // SCAN_CHUNKS
    pad = SUBLANES
    xpad[0:pad, :] = jnp.zeros((pad, C), F32)
    xpad[pad:pad + S, :] = bx_ref[...]
    z = -lam_ref[...]
    softplus = jnp.maximum(z, 0.0) + jnp.log1p(jnp.exp(-jnp.abs(z)))
    cw = cw_ref[...]
    for j in range(SCAN_CHUNKS):
        base = j * chunk
        u = jnp.zeros((chunk, C), F32) + cb_ref[...]
        for t in range(LRU_CONV_KERNEL):
            lo = base + pad - (LRU_CONV_KERNEL - 1) + t
            u = u + cw[t:t + 1, :] * xpad[lo:lo + chunk, :]
        ub = u.astype(BF16)
        r = jax.nn.sigmoid(jnp.dot(ub, wa_ref[...], preferred_element_type=F32) + ba_ref[...])
        gate_i = jax.nn.sigmoid(jnp.dot(ub, wi_ref[...], preferred_element_type=F32) + bi_ref[...])
        log_a = (-LRU_C) * r * softplus
        a = jnp.exp(log_a)
        bterm = jnp.sqrt(jnp.tanh(-log_a) * (1.0 + a * a)) * (gate_i * u)
        for s in range(n_slab):
            a_s[s, j * pitch:j * pitch + chunk, :] = a[:, s * LANES:(s + 1) * LANES]
            b_s[s, j * pitch:j * pitch + chunk, :] = bterm[:, s * LANES:(s + 1) * LANES]

    def step(i, carry):
        out = []
        for s in range(n_slab):
            h, p = carry[2 * s], carry[2 * s + 1]
            idx = pl.ds(i, SCAN_CHUNKS, stride=pitch)
            av = a_s[s, idx, :]
            h = av * h + b_s[s, idx, :]
            p = av * p
            b_s[s, idx, :] = h
            a_s[s, idx, :] = p
            out += [h, p]
        return tuple(out)

    init = []
    for s in range(n_slab):
        init += [jnp.zeros((SCAN_CHUNKS, LANES), F32), jnp.ones((SCAN_CHUNKS, LANES), F32)]
    fin = lax.fori_loop(0, chunk, step, tuple(init), unroll=8)

    for s in range(n_slab):
        h_end, p_end = fin[2 * s], fin[2 * s + 1]
        carry_in = jnp.zeros((1, LANES), F32)
        for j in range(SCAN_CHUNKS):
            rows = slice(j * pitch, j * pitch + chunk)
            h = b_s[s, rows, :] + a_s[s, rows, :] * carry_in
            gate = jax.nn.gelu(bg_ref[j * chunk:(j + 1) * chunk, s * LANES:(s + 1) * LANES])
            o_ref[j * chunk:(j + 1) * chunk, s * LANES:(s + 1) * LANES] = (h * gate).astype(o_ref.dtype)
            carry_in = h_end[j:j + 1, :] + p_end[j:j + 1, :] * carry_in


def _lru_mixer(l, bx, bg, cw, cb, wa, ba, wi, bi, lam):
    B, S, C = bx.shape
    chunk = S // SCAN_CHUNKS
    pitch = chunk + SUBLANES
    seq = pl.BlockSpec((None, S, C), lambda b: (b, 0, 0))
    params = (cw, cb, wa, ba, wi, bi, lam)
    return pl.pallas_call(
        functools.partial(_lru_body, pitch=pitch),
        grid=(B,),
        in_specs=[seq, seq] + [_layer(p, l) for p in params],
        out_specs=seq,
        out_shape=jax.ShapeDtypeStruct((B, S, C), BF16),
        scratch_shapes=[pltpu.VMEM((S + SUBLANES, C), F32),
                        pltpu.VMEM((C // LANES, SCAN_CHUNKS * pitch, LANES), F32),
                        pltpu.VMEM((C // LANES, SCAN_CHUNKS * pitch, LANES), F32)],
        compiler_params=_params("arbitrary"),
        name="lru_mixer",
    )(bx, bg, *params)


def _attn_bias_tables(seq):
    blk = ATT_BLOCK
    local = [(w, d) for w, d in DILATED_PATTERNS if d == 1]
    strided = [(w, d) for w, d in DILATED_PATTERNS if d != 1]
    for window, dil in DILATED_PATTERNS:
        n = window // dil
        assert seq % dil == 0 and (seq // dil) % n == 0, "pattern padding not supported"
    assert all(w <= blk for w, _ in local) and all(d % ATT_PERM == 0 for _, d in strided)
    assert seq % (ATT_PERM * blk) == 0

    def bias(mult):
        return jnp.asarray(np.where(mult > 0, np.log2(np.maximum(mult, 1)), NEG_BIAS), F32)

    r = np.arange(blk)[:, None]
    d = r - np.arange(2 * blk)[None, :] + blk
    near = sum(((d >= 0) & (d <= w)).astype(np.int64) for w, _ in local)
    n_sub = seq // ATT_PERM // blk
    d = ATT_PERM * (r - np.arange(n_sub * blk)[None, :] + (n_sub - 1) * blk)
    far = sum(((d >= 0) & (d % dil == 0) & (d <= w)).astype(np.int64) for w, dil in strided)
    return bias(near), bias(far)


def _scores(q, k):
    return lax.dot_general(q, k, (((1,), (1,)), ((), ())), preferred_element_type=F32)


def _attn_body(q_ref, k_ref, v_ref, qp_ref, kp_ref, vp_ref, near_ref, far_ref, o_ref,
               vn, vp, acc_far, max_far):
    S = q_ref.shape[0]
    blk = ATT_BLOCK
    n_res, sub_len = qp_ref.shape[0], qp_ref.shape[1]
    n_sub = sub_len // blk
    head0 = lax.broadcasted_iota(jnp.int32, (blk, LANES), 1) < HEAD_DIM
    head0_sub = lax.broadcasted_iota(jnp.int32, (sub_len, LANES), 1) < HEAD_DIM
    one = jnp.ones((sub_len, LANES), BF16)
    for h in range(2):
        own = head0_sub if h == 0 else jnp.logical_not(head0_sub)
        for r in range(n_res):
            vp[h, r] = jnp.where(own, vp_ref[r], one)
        for c in range(S // sub_len):
            rows = slice(c * sub_len, (c + 1) * sub_len)
            vn[h, rows, :] = jnp.where(own, v_ref[rows, :], one)

    def own_lanes(h):
        return head0 if h == 0 else jnp.logical_not(head0)

    def far_block(h, r, a):
        nk = (a + 1) * blk

        def scores():
            q = qp_ref[r, a * blk:(a + 1) * blk, :]
            q = jnp.where(own_lanes(h), q, jnp.zeros_like(q))
            return _scores(q, kp_ref[r, 0:nk, :]) + far_ref[:, (n_sub - 1 - a) * blk:]

        def finish(s):
            m = jnp.max(s, axis=-1, keepdims=True)
            p = jnp.exp2(s - m).astype(BF16)
            rows = pl.ds(r + n_res * a * blk, blk, stride=n_res)
            acc_far[h, rows, :] = jnp.dot(p, vp[h, r, 0:nk, :], preferred_element_type=F32)
            max_far[h, rows, :] = jnp.broadcast_to(m, (blk, LANES))

        return scores, finish

    def near_block(i):
        rows = slice(i * blk, (i + 1) * blk)
        keys = slice(max(i - 1, 0) * blk, (i + 1) * blk)
        bias = near_ref[...] if i > 0 else near_ref[:, blk:]

        def scores():
            q = q_ref[rows, :]
            return [_scores(jnp.where(own_lanes(h), q, jnp.zeros_like(q)), k_ref[keys, :]) + bias
                    for h in range(2)]

        def finish(s_heads):
            acc = []
            for h, s in enumerate(s_heads):
                m_far = max_far[h, rows, :]
                m = jnp.maximum(jnp.max(s, axis=-1, keepdims=True), m_far)
                m_keys = m if i == 0 else jnp.concatenate([m, m], axis=-1)
                p = jnp.exp2(s - m_keys).astype(BF16)
                acc.append(jnp.dot(p, vn[h, keys, :], preferred_element_type=F32)
                           + jnp.exp2(m_far - m) * acc_far[h, rows, :])
            num = jnp.where(head0, acc[0], acc[1])
            den = pltpu.roll(jnp.where(head0, acc[1], acc[0]), HEAD_DIM, axis=1)
            o_ref[rows, :] = (num / den).astype(o_ref.dtype)

        return scores, finish

    blocks = [far_block(h, r, a) for r in range(n_res) for a in range(n_sub) for h in range(2)]
    blocks += [near_block(i) for i in range(S // blk)]
    pending = []
    for scores, finish in blocks:
        pending.append((finish, scores()))
        if len(pending) > ATT_PIPELINE:
            fn, s = pending.pop(0)
            fn(s)
    for fn, s in pending:
        fn(s)


def _dilated_attention(q, k, v, qp, kp, vp, near, far):
    B, S, W = q.shape
    n_res, sub_len = qp.shape[1], qp.shape[2]
    nat = pl.BlockSpec((None, S, LANES), lambda b, h: (b, 0, h))
    perm = pl.BlockSpec((None, n_res, sub_len, LANES), lambda b, h: (b, 0, 0, h))
    return pl.pallas_call(
        _attn_body,
        grid=(B, W // LANES),
        in_specs=[nat, nat, nat, perm, perm, perm, _full(near.shape), _full(far.shape)],
        out_specs=nat,
        out_shape=jax.ShapeDtypeStruct((B, S, W), BF16),
        scratch_shapes=[pltpu.VMEM((2, S, LANES), BF16),
                        pltpu.VMEM((2, n_res, sub_len, LANES), BF16),
                        pltpu.VMEM((2, S, LANES), F32),
                        pltpu.VMEM((2, S, LANES), F32)],
        compiler_params=_params("arbitrary", "arbitrary"),
        name="dilated_attn",
    )(q, k, v, qp, kp, vp, near, far)


def _norm_matmul_body(x_ref, g_ref, w_ref, o_ref):
    hb = _rms(x_ref[...], g_ref[...]).astype(BF16)
    o_ref[...] = jnp.dot(hb, w_ref[...], preferred_element_type=F32).astype(o_ref.dtype)


def _mem_kv_proj(l, x, g, w):
    T, D = x.shape
    N = w.shape[-1]
    tm = ROW_TILE
    return pl.pallas_call(
        _norm_matmul_body,
        grid=(T // tm,),
        in_specs=[pl.BlockSpec((tm, D), lambda i: (i, 0)), _layer(g, l), _layer(w, l)],
        out_specs=pl.BlockSpec((tm, N), lambda i: (i, 0)),
        out_shape=jax.ShapeDtypeStruct((T, N), BF16),
        compiler_params=_params("arbitrary"),
        name="mem_kv_proj",
    )(x, g, w)


def _mix_out_mem_body(ya_ref, yb_ref, yc_ref, x_ref, wout_ref, gmix_ref, gpre_ref, wq_ref, kv_ref,
                      wo_ref, gpost_ref, o_ref):
    y = jnp.concatenate([ya_ref[...], yb_ref[...], yc_ref[...]], axis=-1)
    y = jnp.dot(y, wout_ref[...], preferred_element_type=F32)
    x = x_ref[...] + _rms(y, gmix_ref[...])

    D = x.shape[-1]
    hd = D // MEM_HEADS
    hb = _rms(x, gpre_ref[...]).astype(BF16)
    q = jnp.dot(hb, wq_ref[...], preferred_element_type=F32) * (hd ** -0.5)
    q = q.astype(BF16)
    scores = [_scores(q[:, h * hd:(h + 1) * hd], kv_ref[:, h * hd:(h + 1) * hd])
              for h in range(MEM_HEADS)]
    outs = []
    for h, s in enumerate(scores):
        vh = kv_ref[:, D + h * hd:D + (h + 1) * hd]
        m = jnp.max(s, axis=-1, keepdims=True)
        p = jnp.exp(s - m)
        den = jnp.sum(p, axis=-1, keepdims=True)
        o = jnp.dot(p.astype(BF16), vh, preferred_element_type=F32) / den
        outs.append(o.astype(BF16))
    o = jnp.concatenate(outs, axis=-1)
    y = jnp.dot(o, wo_ref[...], preferred_element_type=F32)
    o_ref[...] = x + _rms(y, gpost_ref[...])


def _mix_out_mem(l, ya, yb, yc, x, wout, gmix, gpre, wq, kv, wo, gpost):
    B, S, D = x.shape
    M = kv.shape[1]
    tm = ROW_TILE
    row = lambda n: pl.BlockSpec((None, tm, n), lambda b, s: (b, s, 0))
    return pl.pallas_call(
        _mix_out_mem_body,
        grid=(B, S // tm),
        in_specs=[row(ya.shape[-1]), row(yb.shape[-1]), row(yc.shape[-1]), row(D),
                  _layer(wout, l), _layer(gmix, l), _layer(gpre, l), _layer(wq, l),
                  pl.BlockSpec((None, M, 2 * D), lambda b, s: (b, 0, 0)),
                  _layer(wo, l), _layer(gpost, l)],
        out_specs=row(D),
        out_shape=jax.ShapeDtypeStruct((B, S, D), F32),
        compiler_params=_params("arbitrary", "arbitrary"),
        name="mix_out_mem_attn",
    )(ya, yb, yc, x, wout, gmix, gpre, wq, kv, wo, gpost)


def _ffn_body(x_ref, gpre_ref, wup_ref, cw_ref, cb_ref, wd_ref, gpost_ref, o_ref,
              carry, ubuf_g, ubuf_u, act, *, d_ff, tf):
    tm = x_ref.shape[0]
    pad = SUBLANES

    @pl.when(pl.program_id(1) == 0)
    def _():
        carry[...] = jnp.zeros(carry.shape, F32)

    x = x_ref[...]
    hb = _rms(x, gpre_ref[...]).astype(BF16)

    def conv_branch(col0, ubuf):
        cols = slice(col0, col0 + tf)
        u = jnp.dot(hb, wup_ref[:, cols], preferred_element_type=F32)
        ubuf[0:pad, :] = carry[:, cols]
        ubuf[pad:pad + tm, :] = u
        carry[:, cols] = u[tm - pad:tm, :]
        w = cw_ref[:, cols]
        y = cb_ref[:, cols] + w[FFN_CONV_KERNEL - 1:FFN_CONV_KERNEL, :] * u
        for t in range(FFN_CONV_KERNEL - 1):
            back = FFN_CONV_KERNEL - 1 - t
            y = y + w[t:t + 1, :] * ubuf[pad - back:pad - back + tm, :]
        return y

    for c in range(d_ff // tf):
        gate = conv_branch(c * tf, ubuf_g)
        up = conv_branch(d_ff + c * tf, ubuf_u)
        act[:, c * tf:(c + 1) * tf] = (jax.nn.gelu(gate) * up).astype(BF16)
    y = jnp.dot(act[...], wd_ref[...], preferred_element_type=F32)
    o_ref[...] = x + _rms(y, gpost_ref[...])


def _ffn(l, x, gpre, wup, cw, cb, wd, gpost):
    B, S, D = x.shape
    d_ff = wd.shape[1]
    tm = ROW_TILE
    tf = MXU_DIM
    row = pl.BlockSpec((None, tm, D), lambda b, s: (b, s, 0))
    return pl.pallas_call(
        functools.partial(_ffn_body, d_ff=d_ff, tf=tf),
        grid=(B, S // tm),
        in_specs=[row, _layer(gpre, l), _layer(wup, l), _layer(cw, l), _layer(cb, l),
                  _layer(wd, l), _layer(gpost, l)],
        out_specs=row,
        out_shape=jax.ShapeDtypeStruct((B, S, D), F32),
        scratch_shapes=[pltpu.VMEM((SUBLANES, 2 * d_ff), F32),
                        pltpu.VMEM((SUBLANES + tm, tf), F32),
                        pltpu.VMEM((SUBLANES + tm, tf), F32),
                        pltpu.VMEM((tm, d_ff), BF16)],
        compiler_params=_params("arbitrary", "arbitrary"),
        name="conv_ffn",
    )(x, gpre, wup, cw, cb, wd, gpost)


def _rope_tables(seq, heads):
    inv = 1.0 / (ROPE_THETA ** (jnp.arange(0, HEAD_DIM, 2, dtype=F32) / HEAD_DIM))
    ang = jnp.arange(seq, dtype=F32)[:, None] * inv[None, :]
    cos, sin = jnp.cos(ang), jnp.sin(ang)
    cos_h = jnp.concatenate([cos, cos], axis=-1)
    sin_h = jnp.concatenate([-sin, sin], axis=-1)
    return jnp.tile(cos_h, (1, heads)), jnp.tile(sin_h, (1, heads))


def _block_diag(w):
    depth, h, n, _ = w.shape
    eye = jnp.eye(h, dtype=w.dtype)
    return jnp.einsum('lhij,hg->lhigj', w, eye).reshape(depth, h * n, h * n)


def kernel(x, mem, g_mix_pre, w_in, conv_w, conv_b, conv_ln_g, conv_ln_b, lru_conv_w, lru_conv_b, lru_w_a, lru_b_a, lru_w_i, lru_b_i, lru_lambda, w_out, g_mix_post, g_mem_pre, g_mem_kv, w_mem_q, w_mem_kv, w_mem_o, g_mem_post, g_ffn_pre, w_up, ffn_conv_w, ffn_conv_b, w_down, g_ffn_post):
    B, S, D = x.shape
    depth = w_in.shape[0]
    conv_ch = conv_w.shape[-1]
    lru_w = lru_conv_w.shape[-1]
    att_w = w_out.shape[1] - conv_ch - lru_w
    n_mem = mem.shape[1]
    assert S % ROW_TILE == 0 and ROW_TILE % (ATT_PERM * SUBLANES * 2) == 0
    assert att_w % LANES == 0 and LANES == 2 * HEAD_DIM and (B * n_mem) % ROW_TILE == 0

    cos, sin = _rope_tables(S, att_w // HEAD_DIM)
    near, far = _attn_bias_tables(S)
    vec = lambda p: p.reshape(depth, 1, -1)
    mxu = lambda p: p.astype(BF16)
    mem2 = mem.reshape(B * n_mem, D)

    g_mix_pre, g_mix_post, g_mem_pre, g_mem_kv, g_mem_post, g_ffn_pre, g_ffn_post = map(
        vec, (g_mix_pre, g_mix_post, g_mem_pre, g_mem_kv, g_mem_post, g_ffn_pre, g_ffn_post))
    conv_b, conv_ln_g, conv_ln_b, lru_conv_b, lru_b_a, lru_b_i, lru_lambda, ffn_conv_b = map(
        vec, (conv_b, conv_ln_g, conv_ln_b, lru_conv_b, lru_b_a, lru_b_i, lru_lambda, ffn_conv_b))
    w_in, w_out, w_mem_q, w_mem_kv, w_mem_o, w_up, w_down = map(
        mxu, (w_in, w_out, w_mem_q, w_mem_kv, w_mem_o, w_up, w_down))
    lru_w_a = mxu(_block_diag(lru_w_a))
    lru_w_i = mxu(_block_diag(lru_w_i))

    for l in range(depth):
        y_a, bx, bg, q, k, v, qp, kp, vp = _in_proj(
            l, x, g_mix_pre, w_in, cos, sin, conv_w, conv_b, conv_ln_g, conv_ln_b, conv_ch, lru_w, att_w)
        y_b = _lru_mixer(l, bx, bg, lru_conv_w, lru_conv_b, lru_w_a, lru_b_a, lru_w_i, lru_b_i,
                         lru_lambda)
        y_c = _dilated_attention(q, k, v, qp, kp, vp, near, far)
        kv = _mem_kv_proj(l, mem2, g_mem_kv, w_mem_kv).reshape(B, n_mem, 2 * D)
        x = _mix_out_mem(l, y_a, y_b, y_c, x, w_out, g_mix_post, g_mem_pre, w_mem_q, kv, w_mem_o,
                         g_mem_post)
        x = _ffn(l, x, g_ffn_pre, w_up, ffn_conv_w, ffn_conv_b, w_down, g_ffn_post)
    return x
```

```python
import functools
import math

import numpy as np
import jax
import jax.numpy as jnp
from jax import lax
from jax.experimental import pallas as pl
from jax.experimental.pallas import tpu as pltpu

F32 = jnp.float32
BF16 = jnp.bfloat16

HEAD_DIM = 64
CONV_KERNEL = 31
LRU_CONV_KERNEL = 4
LRU_C = 8.0
DILATED_PATTERNS = ((128, 1), (512, 4), (2048, 16))
ROPE_THETA = 10000.0
MEM_HEADS = 4
FFN_CONV_KERNEL = 3
EPS = 1e-6

LANES = 128
SUBLANES = 8
MXU_DIM = 256
VMEM_LIMIT_BYTES = 56 * 1024 * 1024

ROW_TILE = 512
ATT_BLOCK = 128
ATT_PERM = 4
ATT_PIPELINE = 4
CONV_HALO = 32
CONV_SUB = 64
LOG2E = 1.4426950408889634
SCAN_CHUNKS = SUBLANES
NEG_BIAS = -1e30


def _params(*sem):
    return pltpu.CompilerParams(dimension_semantics=sem, vmem_limit_bytes=VMEM_LIMIT_BYTES)


def _rms(x, g):
    return x * lax.rsqrt(jnp.mean(x * x, axis=-1, keepdims=True) + EPS) * g


def _sigmoid(x):
    return 1.0 / (1.0 + jnp.exp2(x * (-LOG2E)))


def _gelu_tanh(x):
    k = -2.0 * math.sqrt(2.0 / math.pi) * LOG2E
    return x / (1.0 + jnp.exp2(x * (k + (0.044715 * k) * (x * x))))


def _full(shape):
    n = len(shape)
    return pl.BlockSpec(shape, lambda *_: (0,) * n)


def _layer(p, l):
    return pl.BlockSpec((None,) + p.shape[1:], lambda *_: (l, 0, 0), pipeline_mode=pl.Buffered(1))


def _cast_weights_once(grid_rank, pairs):
    first = pl.program_id(0) == 0
    for axis in range(1, grid_rank):
        first = jnp.logical_and(first, pl.program_id(axis) == 0)

    @pl.when(first)
    def _():
        for src, dst in pairs:
            dst[...] = src[...].astype(dst.dtype)


def _mxu_copy(w):
    return pltpu.VMEM(w.shape[1:], BF16)


def _rope(t, cos, sin_signed, first_half):
    n = t.shape[-1]
    half = HEAD_DIM // 2
    partner = jnp.where(first_half, pltpu.roll(t, n - half, axis=1), pltpu.roll(t, half, axis=1))
    return t * cos + partner * sin_signed


def _store_attn_operand(t, nat_ref, perm_ref, stage):
    rows, width = t.shape
    nat_ref[...] = t.astype(BF16)
    for c in range(width // LANES):
        stage[c] = t[:, c * LANES:(c + 1) * LANES]
    for c in range(width // LANES):
        for r in range(ATT_PERM):
            sub = stage[c, pl.ds(r, rows // ATT_PERM, stride=ATT_PERM), :]
            perm_ref[r, :, c * LANES:(c + 1) * LANES] = sub.astype(BF16)


def _conv_mixer_halo(ubuf, tm):
    @pl.when(pl.program_id(1) == 0)
    def _():
        ubuf[0:CONV_HALO, :] = jnp.zeros((CONV_HALO, ubuf.shape[-1]), F32)

    @pl.when(pl.program_id(1) > 0)
    def _():
        ubuf[0:CONV_HALO, :] = ubuf[tm:tm + CONV_HALO, :]


def _conv_mixer_prepare(a, ubuf, ush, *, ch):
    tm = a.shape[0]
    ubuf[CONV_HALO:CONV_HALO + tm, :] = a[:, :ch] * _sigmoid(a[:, ch:])
    span = CONV_HALO + tm - SUBLANES
    for s in range(1, SUBLANES):
        ush[s - 1, 0:span, :] = ubuf[s:s + span, :]


def _conv_mixer_rows(r0, w_ref, b_ref, g_ref, beta_ref, o_ref, ubuf, ush):
    ch = ubuf.shape[-1]
    w = w_ref[...]
    off = CONV_HALO - (CONV_KERNEL - 1)
    acc = jnp.zeros((CONV_SUB, ch), F32) + b_ref[...]
    for j in range(CONV_KERNEL):
        s = (off + j) % SUBLANES
        lo = r0 + off + j - s
        win = ubuf[lo:lo + CONV_SUB, :] if s == 0 else ush[s - 1, lo:lo + CONV_SUB, :]
        acc = acc + w[j:j + 1, :] * win
    mu = jnp.mean(acc, axis=-1, keepdims=True)
    cen = acc - mu
    var = jnp.mean(cen * cen, axis=-1, keepdims=True)
    y = cen * lax.rsqrt(var + EPS) * g_ref[...] + beta_ref[...]
    o_ref[r0:r0 + CONV_SUB, :] = (y * _sigmoid(y)).astype(o_ref.dtype)


def _in_proj_body(x_ref, g_ref, w32_ref, cos_ref, sin_ref, cw_ref, cb_ref, cg_ref, cbeta_ref,
                  ya_ref, bx_ref, bg_ref, q_ref, k_ref, v_ref, qp_ref, kp_ref, vp_ref,
                  w_ref, ubuf, ush, stage_q, stage_k, stage_v, *, splits):
    tm = x_ref.shape[0]
    _cast_weights_once(2, [(w32_ref, w_ref)])
    _conv_mixer_halo(ubuf, tm)
    hb = _rms(x_ref[...], g_ref[...]).astype(BF16)

    def proj(lo, hi):
        return jnp.dot(hb, w_ref[:, lo:hi], preferred_element_type=F32)

    c_a, c_bx, c_bg, c_q, c_k, c_v = splits
    rows = pl.ds(pl.multiple_of(pl.program_id(1) * tm, tm), tm)
    cos = cos_ref[rows, :]
    sin = sin_ref[rows, :]
    lane = lax.broadcasted_iota(jnp.int32, cos.shape, 1)
    first_half = (lane % HEAD_DIM) < (HEAD_DIM // 2)
    _conv_mixer_prepare(proj(0, c_a), ubuf, ush, ch=c_a // 2)
    for r0 in range(0, tm, CONV_SUB):
        _conv_mixer_rows(r0, cw_ref, cb_ref, cg_ref, cbeta_ref, ya_ref, ubuf, ush)
    bx_ref[...] = proj(c_a, c_bx)
    bg_ref[...] = proj(c_bx, c_bg)
    q = _rope(proj(c_bg, c_q), cos, sin, first_half) * (HEAD_DIM ** -0.5 * LOG2E)
    _store_attn_operand(q, q_ref, qp_ref, stage_q)
    _store_attn_operand(_rope(proj(c_q, c_k), cos, sin, first_half), k_ref, kp_ref, stage_k)
    _store_attn_operand(proj(c_k, c_v), v_ref, vp_ref, stage_v)


def _in_proj(l, x, g, w, cos, sin, conv_w, conv_b, conv_g, conv_beta, conv_ch, lru_w, att_w):
    B, S, D = x.shape
    tm = ROW_TILE
    splits = (2 * conv_ch, 2 * conv_ch + lru_w, 2 * conv_ch + 2 * lru_w,
              2 * conv_ch + 2 * lru_w + att_w, 2 * conv_ch + 2 * lru_w + 2 * att_w,
              2 * conv_ch + 2 * lru_w + 3 * att_w)
    row = lambda n: pl.BlockSpec((None, tm, n), lambda b, s: (b, s, 0))
    perm = pl.BlockSpec((None, ATT_PERM, tm // ATT_PERM, att_w), lambda b, s: (b, 0, s, 0))
    nat_shape = jax.ShapeDtypeStruct((B, S, att_w), BF16)
    perm_shape = jax.ShapeDtypeStruct((B, ATT_PERM, S // ATT_PERM, att_w), BF16)
    stage = pltpu.VMEM((att_w // LANES, tm, LANES), F32)
    return pl.pallas_call(
        functools.partial(_in_proj_body, splits=splits),
        grid=(B, S // tm),
        in_specs=[row(D), _layer(g, l), _layer(w, l), _full(cos.shape), _full(sin.shape),
                  _layer(conv_w, l), _layer(conv_b, l), _layer(conv_g, l), _layer(conv_beta, l)],
        out_specs=[row(conv_ch), row(lru_w), row(lru_w), row(att_w), row(att_w), row(att_w),
                   perm, perm, perm],
        out_shape=[jax.ShapeDtypeStruct((B, S, conv_ch), BF16),
                   jax.ShapeDtypeStruct((B, S, lru_w), F32),
                   jax.ShapeDtypeStruct((B, S, lru_w), F32),
                   nat_shape, nat_shape, nat_shape, perm_shape, perm_shape, perm_shape],
        scratch_shapes=[_mxu_copy(w),
                        pltpu.VMEM((CONV_HALO + tm, conv_ch), F32),
                        pltpu.VMEM((SUBLANES - 1, CONV_HALO + tm - SUBLANES, conv_ch), F32),
                        stage, stage, stage],
        compiler_params=_params("arbitrary", "arbitrary"),
        name="in_proj",
    )(x, g, w, cos, sin, conv_w, conv_b, conv_g, conv_beta)


def _lru_body(bx_ref, bg_ref, cw_ref, cb_ref, wa_ref, ba_ref, wi_ref, bi_ref, lam_ref, o_ref,
              xpad, a_s, b_s, *, pitch):
    S, C = bx_ref.shape
    n_slab = C // LANES
    chunk = S // SCAN_CHUNKS
    pad = SUBLANES
    xpad[0:pad, :] = jnp.zeros((pad, C), F32)
    xpad[pad:pad + S, :] = bx_ref[...]
    z = -lam_ref[...]
    softplus = jnp.maximum(z, 0.0) + jnp.log1p(jnp.exp(-jnp.abs(z)))
    cw = cw_ref[...]
    for j in range(SCAN_CHUNKS):
        base = j * chunk
        u = jnp.zeros((chunk, C), F32) + cb_ref[...]
        for t in range(LRU_CONV_KERNEL):
            lo = base + pad - (LRU_CONV_KERNEL - 1) + t
            u = u + cw[t:t + 1, :] * xpad[lo:lo + chunk, :]
        ub = u.astype(BF16)
        r = _sigmoid(jnp.dot(ub, wa_ref[...], preferred_element_type=F32) + ba_ref[...])
        gate_i = _sigmoid(jnp.dot(ub, wi_ref[...], preferred_element_type=F32) + bi_ref[...])
        log_a = (-LRU_C) * r * softplus
        a = jnp.exp(log_a)
        bterm = jnp.sqrt(jnp.tanh(-log_a) * (1.0 + a * a)) * (gate_i * u)
        for s in range(n_slab):
            a_s[s, j * pitch:j * pitch + chunk, :] = a[:, s * LANES:(s + 1) * LANES]
            b_s[s, j * pitch:j * pitch + chunk, :] = bterm[:, s * LANES:(s + 1) * LANES]

    def step(i, carry):
        out = []
        for s in range(n_slab):
            h, p = carry[2 * s], carry[2 * s + 1]
            idx = pl.ds(i, SCAN_CHUNKS, stride=pitch)
            av = a_s[s, idx, :]
            h = av * h + b_s[s, idx, :]
            p = av * p
            b_s[s, idx, :] = h
            a_s[s, idx, :] = p
            out += [h, p]
        return tuple(out)

    init = []
    for s in range(n_slab):
        init += [jnp.zeros((SCAN_CHUNKS, LANES), F32), jnp.ones((SCAN_CHUNKS, LANES), F32)]
    fin = lax.fori_loop(0, chunk, step, tuple(init), unroll=8)

    for s in range(n_slab):
        h_end, p_end = fin[2 * s], fin[2 * s + 1]
        carry_in = jnp.zeros((1, LANES), F32)
        for j in range(SCAN_CHUNKS):
            rows = slice(j * pitch, j * pitch + chunk)
            h = b_s[s, rows, :] + a_s[s, rows, :] * carry_in
            gate = _gelu_tanh(bg_ref[j * chunk:(j + 1) * chunk, s * LANES:(s + 1) * LANES])
            o_ref[j * chunk:(j + 1) * chunk, s * LANES:(s + 1) * LANES] = (h * gate).astype(o_ref.dtype)
            carry_in = h_end[j:j + 1, :] + p_end[j:j + 1, :] * carry_in


def _lru_mixer(l, bx, bg, cw, cb, wa, ba, wi, bi, lam):
    B, S, C = bx.shape
    chunk = S // SCAN_CHUNKS
    pitch = chunk + SUBLANES
    seq = pl.BlockSpec((None, S, C), lambda b: (b, 0, 0))
    params = (cw, cb, wa, ba, wi, bi, lam)
    return pl.pallas_call(
        functools.partial(_lru_body, pitch=pitch),
        grid=(B,),
        in_specs=[seq, seq] + [_layer(p, l) for p in params],
        out_specs=seq,
        out_shape=jax.ShapeDtypeStruct((B, S, C), BF16),
        scratch_shapes=[pltpu.VMEM((S + SUBLANES, C), F32),
                        pltpu.VMEM((C // LANES, SCAN_CHUNKS * pitch, LANES), F32),
                        pltpu.VMEM((C // LANES, SCAN_CHUNKS * pitch, LANES), F32)],
        compiler_params=_params("arbitrary"),
        name="lru_mixer",
    )(bx, bg, *params)


def _attn_bias_tables(seq):
    blk = ATT_BLOCK
    local = [(w, d) for w, d in DILATED_PATTERNS if d == 1]
    strided = [(w, d) for w, d in DILATED_PATTERNS if d != 1]
    for window, dil in DILATED_PATTERNS:
        n = window // dil
        assert seq % dil == 0 and (seq // dil) % n == 0, "pattern padding not supported"
    assert all(w <= blk for w, _ in local) and all(d % ATT_PERM == 0 for _, d in strided)
    assert seq % (ATT_PERM * blk) == 0

    def bias(mult):
        return jnp.asarray(np.where(mult > 0, np.log2(np.maximum(mult, 1)), NEG_BIAS), F32)

    r = np.arange(blk)[:, None]
    d = r - np.arange(2 * blk)[None, :] + blk
    near = sum(((d >= 0) & (d <= w)).astype(np.int64) for w, _ in local)
    n_sub = seq // ATT_PERM // blk
    d = ATT_PERM * (r - np.arange(n_sub * blk)[None, :] + (n_sub - 1) * blk)
    far = sum(((d >= 0) & (d % dil == 0) & (d <= w)).astype(np.int64) for w, dil in strided)
    return bias(near), bias(far)


def _scores(q, k):
    return lax.dot_general(q, k, (((1,), (1,)), ((), ())), preferred_element_type=F32)


def _attn_body(q_ref, k_ref, v_ref, qp_ref, kp_ref, vp_ref, near_ref, far_ref, o_ref,
               vn, vp, acc_far, max_far):
    S = q_ref.shape[0]
    blk = ATT_BLOCK
    n_res, sub_len = qp_ref.shape[0], qp_ref.shape[1]
    n_sub = sub_len // blk
    head0 = lax.broadcasted_iota(jnp.int32, (blk, LANES), 1) < HEAD_DIM
    head0_sub = lax.broadcasted_iota(jnp.int32, (sub_len, LANES), 1) < HEAD_DIM
    one = jnp.ones((sub_len, LANES), BF16)
    for h in range(2):
        own = head0_sub if h == 0 else jnp.logical_not(head0_sub)
        for r in range(n_res):
            vp[h, r] = jnp.where(own, vp_ref[r], one)
        for c in range(S // sub_len):
            rows = slice(c * sub_len, (c + 1) * sub_len)
            vn[h, rows, :] = jnp.where(own, v_ref[rows, :], one)

    def own_lanes(h):
        return head0 if h == 0 else jnp.logical_not(head0)

    def far_block(h, r, a):
        nk = (a + 1) * blk

        def scores():
            q = qp_ref[r, a * blk:(a + 1) * blk, :]
            q = jnp.where(own_lanes(h), q, jnp.zeros_like(q))
            return _scores(q, kp_ref[r, 0:nk, :]) + far_ref[:, (n_sub - 1 - a) * blk:]

        def finish(s):
            m = jnp.max(s, axis=-1, keepdims=True)
            p = jnp.exp2(s - m).astype(BF16)
            rows = pl.ds(r + n_res * a * blk, blk, stride=n_res)
            acc_far[h, rows, :] = jnp.dot(p, vp[h, r, 0:nk, :], preferred_element_type=F32)
            max_far[h, rows, :] = jnp.broadcast_to(m, (blk, LANES))

        return scores, finish

    def near_block(i):
        rows = slice(i * blk, (i + 1) * blk)
        keys = slice(max(i - 1, 0) * blk, (i + 1) * blk)
        bias = near_ref[...] if i > 0 else near_ref[:, blk:]

        def scores():
            q = q_ref[rows, :]
            return [_scores(jnp.where(own_lanes(h), q, jnp.zeros_like(q)), k_ref[keys, :]) + bias
                    for h in range(2)]

        def finish(s_heads):
            acc = []
            for h, s in enumerate(s_heads):
                m_far = max_far[h, rows, :]
                m = jnp.maximum(jnp.max(s, axis=-1, keepdims=True), m_far)
                m_keys = m if i == 0 else jnp.concatenate([m, m], axis=-1)
                p = jnp.exp2(s - m_keys).astype(BF16)
                acc.append(jnp.dot(p, vn[h, keys, :], preferred_element_type=F32)
                           + jnp.exp2(m_far - m) * acc_far[h, rows, :])
            num = jnp.where(head0, acc[0], acc[1])
            den = pltpu.roll(jnp.where(head0, acc[1], acc[0]), HEAD_DIM, axis=1)
            o_ref[rows, :] = (num / den).astype(o_ref.dtype)

        return scores, finish

    blocks = [far_block(h, r, a) for r in range(n_res) for a in range(n_sub) for h in range(2)]
    blocks += [near_block(i) for i in range(S // blk)]
    pending = []
    for scores, finish in blocks:
        pending.append((finish, scores()))
        if len(pending) > ATT_PIPELINE:
            fn, s = pending.pop(0)
            fn(s)
    for fn, s in pending:
        fn(s)


def _dilated_attention(q, k, v, qp, kp, vp, near, far):
    B, S, W = q.shape
    n_res, sub_len = qp.shape[1], qp.shape[2]
    nat = pl.BlockSpec((None, S, LANES), lambda b, h: (b, 0, h))
    perm = pl.BlockSpec((None, n_res, sub_len, LANES), lambda b, h: (b, 0, 0, h))
    return pl.pallas_call(
        _attn_body,
        grid=(B, W // LANES),
        in_specs=[nat, nat, nat, perm, perm, perm, _full(near.shape), _full(far.shape)],
        out_specs=nat,
        out_shape=jax.ShapeDtypeStruct((B, S, W), BF16),
        scratch_shapes=[pltpu.VMEM((2, S, LANES), BF16),
                        pltpu.VMEM((2, n_res, sub_len, LANES), BF16),
                        pltpu.VMEM((2, S, LANES), F32),
                        pltpu.VMEM((2, S, LANES), F32)],
        compiler_params=_params("arbitrary", "arbitrary"),
        name="dilated_attn",
    )(q, k, v, qp, kp, vp, near, far)


def _norm_matmul_body(x_ref, g_ref, w32_ref, o_ref, w_ref):
    _cast_weights_once(1, [(w32_ref, w_ref)])
    hb = _rms(x_ref[...], g_ref[...]).astype(BF16)
    o_ref[...] = jnp.dot(hb, w_ref[...], preferred_element_type=F32).astype(o_ref.dtype)


def _mem_kv_proj(l, x, g, w):
    T, D = x.shape
    N = w.shape[-1]
    tm = ROW_TILE
    return pl.pallas_call(
        _norm_matmul_body,
        grid=(T // tm,),
        in_specs=[pl.BlockSpec((tm, D), lambda i: (i, 0)), _layer(g, l), _layer(w, l)],
        out_specs=pl.BlockSpec((tm, N), lambda i: (i, 0)),
        out_shape=jax.ShapeDtypeStruct((T, N), BF16),
        scratch_shapes=[_mxu_copy(w)],
        compiler_params=_params("arbitrary"),
        name="mem_kv_proj",
    )(x, g, w)


def _mix_out_mem_body(ya_ref, yb_ref, yc_ref, x_ref, wout32_ref, gmix_ref, gpre_ref, wq32_ref, kv_ref,
                      wo32_ref, gpost_ref, o_ref, wout_ref, wq_ref, wo_ref):
    _cast_weights_once(2, [(wout32_ref, wout_ref), (wq32_ref, wq_ref), (wo32_ref, wo_ref)])
    y = jnp.concatenate([ya_ref[...], yb_ref[...], yc_ref[...]], axis=-1)
    y = jnp.dot(y, wout_ref[...], preferred_element_type=F32)
    x = x_ref[...] + _rms(y, gmix_ref[...])

    D = x.shape[-1]
    hd = D // MEM_HEADS
    hb = _rms(x, gpre_ref[...]).astype(BF16)
    q = jnp.dot(hb, wq_ref[...], preferred_element_type=F32) * (hd ** -0.5)
    q = q.astype(BF16)
    scores = [_scores(q[:, h * hd:(h + 1) * hd], kv_ref[:, h * hd:(h + 1) * hd])
              for h in range(MEM_HEADS)]
    outs = []
    for h, s in enumerate(scores):
        vh = kv_ref[:, D + h * hd:D + (h + 1) * hd]
        m = jnp.max(s, axis=-1, keepdims=True)
        p = jnp.exp(s - m)
        den = jnp.sum(p, axis=-1, keepdims=True)
        o = jnp.dot(p.astype(BF16), vh, preferred_element_type=F32) / den
        outs.append(o.astype(BF16))
    o = jnp.concatenate(outs, axis=-1)
    y = jnp.dot(o, wo_ref[...], preferred_element_type=F32)
    o_ref[...] = x + _rms(y, gpost_ref[...])


def _mix_out_mem(l, ya, yb, yc, x, wout, gmix, gpre, wq, kv, wo, gpost):
    B, S, D = x.shape
    M = kv.shape[1]
    tm = ROW_TILE
    row = lambda n: pl.BlockSpec((None, tm, n), lambda b, s: (b, s, 0))
    return pl.pallas_call(
        _mix_out_mem_body,
        grid=(B, S // tm),
        in_specs=[row(ya.shape[-1]), row(yb.shape[-1]), row(yc.shape[-1]), row(D),
                  _layer(wout, l), _layer(gmix, l), _layer(gpre, l), _layer(wq, l),
                  pl.BlockSpec((None, M, 2 * D), lambda b, s: (b, 0, 0)),
                  _layer(wo, l), _layer(gpost, l)],
        out_specs=row(D),
        out_shape=jax.ShapeDtypeStruct((B, S, D), F32),
        scratch_shapes=[_mxu_copy(wout), _mxu_copy(wq), _mxu_copy(wo)],
        compiler_params=_params("arbitrary", "arbitrary"),
        name="mix_out_mem_attn",
    )(ya, yb, yc, x, wout, gmix, gpre, wq, kv, wo, gpost)


def _ffn_body(x_ref, gpre_ref, wup_ref, cw_ref, cb_ref, wd_ref, gpost_ref, o_ref,
              carry, act, *, d_ff, tf):
    tm = x_ref.shape[0]
    pad = SUBLANES
    n_chunks = d_ff // tf
    first_rows = lax.broadcasted_iota(jnp.int32, (pad, tf), 0)

    @pl.when(pl.program_id(1) == 0)
    def _():
        carry[...] = jnp.zeros(carry.shape, F32)

    x = x_ref[...]
    hb = _rms(x, gpre_ref[...]).astype(BF16)

    def conv(col0):
        cols = slice(col0, col0 + tf)
        u = jnp.dot(hb, wup_ref[:, cols], preferred_element_type=F32)
        last = carry[:, cols]
        carry[:, cols] = u[tm - pad:tm, :]
        w = cw_ref[:, cols]
        y = cb_ref[:, cols] + w[FFN_CONV_KERNEL - 1:FFN_CONV_KERNEL, :] * u
        for t in range(FFN_CONV_KERNEL - 1):
            back = FFN_CONV_KERNEL - 1 - t
            shifted = pltpu.roll(u, back, axis=0)
            head = jnp.where(first_rows < back, pltpu.roll(last, back, axis=0), shifted[0:pad, :])
            y = y + w[t:t + 1, :] * jnp.concatenate([head, shifted[pad:, :]], axis=0)
        return y

    for c in range(n_chunks):
        gate = conv(c * tf)
        up = conv(d_ff + c * tf)
        act[:, c * tf:(c + 1) * tf] = (_gelu_tanh(gate) * up).astype(BF16)
    y = jnp.dot(act[...], wd_ref[...], preferred_element_type=F32)
    o_ref[...] = x + _rms(y, gpost_ref[...])


def _ffn(l, x, gpre, wup, cw, cb, wd, gpost):
    B, S, D = x.shape
    d_ff = wd.shape[1]
    tm = ROW_TILE
    tf = MXU_DIM
    row = pl.BlockSpec((None, tm, D), lambda b, s: (b, s, 0))
    return pl.pallas_call(
        functools.partial(_ffn_body, d_ff=d_ff, tf=tf),
        grid=(B, S // tm),
        in_specs=[row, _layer(gpre, l), _layer(wup, l), _layer(cw, l), _layer(cb, l),
                  _layer(wd, l), _layer(gpost, l)],
        out_specs=row,
        out_shape=jax.ShapeDtypeStruct((B, S, D), F32),
        scratch_shapes=[pltpu.VMEM((SUBLANES, 2 * d_ff), F32),
                        pltpu.VMEM((tm, d_ff), BF16)],
        compiler_params=_params("arbitrary", "arbitrary"),
        name="conv_ffn",
    )(x, gpre, wup, cw, cb, wd, gpost)


def _rope_tables(seq, heads):
    inv = 1.0 / (ROPE_THETA ** (jnp.arange(0, HEAD_DIM, 2, dtype=F32) / HEAD_DIM))
    ang = jnp.arange(seq, dtype=F32)[:, None] * inv[None, :]
    cos, sin = jnp.cos(ang), jnp.sin(ang)
    cos_h = jnp.concatenate([cos, cos], axis=-1)
    sin_h = jnp.concatenate([-sin, sin], axis=-1)
    return jnp.tile(cos_h, (1, heads)), jnp.tile(sin_h, (1, heads))


def _block_diag(w):
    depth, h, n, _ = w.shape
    eye = jnp.eye(h, dtype=w.dtype)
    return jnp.einsum('lhij,hg->lhigj', w, eye).reshape(depth, h * n, h * n)


def kernel(x, mem, g_mix_pre, w_in, conv_w, conv_b, conv_ln_g, conv_ln_b, lru_conv_w, lru_conv_b, lru_w_a, lru_b_a, lru_w_i, lru_b_i, lru_lambda, w_out, g_mix_post, g_mem_pre, g_mem_kv, w_mem_q, w_mem_kv, w_mem_o, g_mem_post, g_ffn_pre, w_up, ffn_conv_w, ffn_conv_b, w_down, g_ffn_post):
    B, S, D = x.shape
    depth = w_in.shape[0]
    conv_ch = conv_w.shape[-1]
    lru_w = lru_conv_w.shape[-1]
    att_w = w_out.shape[1] - conv_ch - lru_w
    n_mem = mem.shape[1]
    assert S % ROW_TILE == 0 and ROW_TILE % (ATT_PERM * SUBLANES * 2) == 0
    assert att_w % LANES == 0 and LANES == 2 * HEAD_DIM and (B * n_mem) % ROW_TILE == 0

    cos, sin = _rope_tables(S, att_w // HEAD_DIM)
    near, far = _attn_bias_tables(S)
    vec = lambda p: p.reshape(depth, 1, -1)
    mxu = lambda p: p.astype(BF16)
    mem2 = mem.reshape(B * n_mem, D)

    g_mix_pre, g_mix_post, g_mem_pre, g_mem_kv, g_mem_post, g_ffn_pre, g_ffn_post = map(
        vec, (g_mix_pre, g_mix_post, g_mem_pre, g_mem_kv, g_mem_post, g_ffn_pre, g_ffn_post))
    conv_b, conv_ln_g, conv_ln_b, lru_conv_b, lru_b_a, lru_b_i, lru_lambda, ffn_conv_b = map(
        vec, (conv_b, conv_ln_g, conv_ln_b, lru_conv_b, lru_b_a, lru_b_i, lru_lambda, ffn_conv_b))
    w_up, w_down = mxu(w_up), mxu(w_down)
    lru_w_a = mxu(_block_diag(lru_w_a))
    lru_w_i = mxu(_block_diag(lru_w_i))

    for l in range(depth):
        y_a, bx, bg, q, k, v, qp, kp, vp = _in_proj(
            l, x, g_mix_pre, w_in, cos, sin, conv_w, conv_b, conv_ln_g, conv_ln_b, conv_ch, lru_w, att_w)
        y_b = _lru_mixer(l, bx, bg, lru_conv_w, lru_conv_b, lru_w_a, lru_b_a, lru_w_i, lru_b_i,
                         lru_lambda)
        y_c = _dilated_attention(q, k, v, qp, kp, vp, near, far)
        kv = _mem_kv_proj(l, mem2, g_mem_kv, w_mem_kv).reshape(B, n_mem, 2 * D)
        x = _mix_out_mem(l, y_a, y_b, y_c, x, w_out, g_mix_post, g_mem_pre, w_mem_q, kv, w_mem_o,
                         g_mem_post)
        x = _ffn(l, x, g_ffn_pre, w_up, ffn_conv_w, ffn_conv_b, w_down, g_ffn_post)
    return x
```

```python
import functools
import math

import numpy as np
import jax
import jax.numpy as jnp
from jax import lax
from jax.experimental import pallas as pl
from jax.experimental.pallas import tpu as pltpu

F32 = jnp.float32
BF16 = jnp.bfloat16

HEAD_DIM = 64
CONV_KERNEL = 31
LRU_CONV_KERNEL = 4
LRU_C = 8.0
DILATED_PATTERNS = ((128, 1), (512, 4), (2048, 16))
ROPE_THETA = 10000.0
MEM_HEADS = 4
FFN_CONV_KERNEL = 3
EPS = 1e-6

LANES = 128
SUBLANES = 8
MXU_DIM = 256
VMEM_LIMIT_BYTES = 56 * 1024 * 1024

ROW_TILE = 512
ATT_BLOCK = 128
ATT_PERM = 4
ATT_PIPELINE = 4
CONV_HALO = 32
CONV_SUB = 64
MIX_ROW_GROUPS = 2
LOG2E = 1.4426950408889634
SCAN_CHUNKS = SUBLANES
NEG_BIAS = -1e30


def _params(*sem):
    return pltpu.CompilerParams(dimension_semantics=sem, vmem_limit_bytes=VMEM_LIMIT_BYTES)


def _rms(x, g):
    return x * lax.rsqrt(jnp.mean(x * x, axis=-1, keepdims=True) + EPS) * g


def _sigmoid(x):
    return 1.0 / (1.0 + jnp.exp2(x * (-LOG2E)))


def _gelu_tanh(x):
    k = -2.0 * math.sqrt(2.0 / math.pi) * LOG2E
    return x / (1.0 + jnp.exp2(x * (k + (0.044715 * k) * (x * x))))


def _full(shape):
    n = len(shape)
    return pl.BlockSpec(shape, lambda *_: (0,) * n)


def _layer(p, l):
    return pl.BlockSpec((None,) + p.shape[1:], lambda *_: (l, 0, 0), pipeline_mode=pl.Buffered(1))


def _cast_weights_once(grid_rank, pairs):
    first = pl.program_id(0) == 0
    for axis in range(1, grid_rank):
        first = jnp.logical_and(first, pl.program_id(axis) == 0)

    @pl.when(first)
    def _():
        for src, dst in pairs:
            dst[...] = src[...].astype(dst.dtype)


def _mxu_copy(w):
    return pltpu.VMEM(w.shape[1:], BF16)


def _rope(t, cos, sin_signed, first_half):
    n = t.shape[-1]
    half = HEAD_DIM // 2
    partner = jnp.where(first_half, pltpu.roll(t, n - half, axis=1), pltpu.roll(t, half, axis=1))
    return t * cos + partner * sin_signed


def _store_attn_operand(t, nat_ref, perm_ref, stage):
    rows, width = t.shape
    nat_ref[...] = t.astype(BF16)
    for c in range(width // LANES):
        stage[c] = t[:, c * LANES:(c + 1) * LANES]
    for c in range(width // LANES):
        for r in range(ATT_PERM):
            sub = stage[c, pl.ds(r, rows // ATT_PERM, stride=ATT_PERM), :]
            perm_ref[r, :, c * LANES:(c + 1) * LANES] = sub.astype(BF16)


def _conv_mixer_halo(ubuf, tm):
    @pl.when(pl.program_id(1) == 0)
    def _():
        ubuf[0:CONV_HALO, :] = jnp.zeros((CONV_HALO, ubuf.shape[-1]), F32)

    @pl.when(pl.program_id(1) > 0)
    def _():
        ubuf[0:CONV_HALO, :] = ubuf[tm:tm + CONV_HALO, :]


def _conv_mixer_prepare(a, ubuf, ush, *, ch):
    tm = a.shape[0]
    ubuf[CONV_HALO:CONV_HALO + tm, :] = a[:, :ch] * _sigmoid(a[:, ch:])
    span = CONV_HALO + tm - SUBLANES
    for s in range(1, SUBLANES):
        ush[s - 1, 0:span, :] = ubuf[s:s + span, :]


def _conv_mixer_rows(r0, w_ref, b_ref, g_ref, beta_ref, o_ref, ubuf, ush):
    ch = ubuf.shape[-1]
    w = w_ref[...]
    off = CONV_HALO - (CONV_KERNEL - 1)
    acc = jnp.zeros((CONV_SUB, ch), F32) + b_ref[...]
    for j in range(CONV_KERNEL):
        s = (off + j) % SUBLANES
        lo = r0 + off + j - s
        win = ubuf[lo:lo + CONV_SUB, :] if s == 0 else ush[s - 1, lo:lo + CONV_SUB, :]
        acc = acc + w[j:j + 1, :] * win
    mu = jnp.mean(acc, axis=-1, keepdims=True)
    cen = acc - mu
    var = jnp.mean(cen * cen, axis=-1, keepdims=True)
    y = cen * lax.rsqrt(var + EPS) * g_ref[...] + beta_ref[...]
    o_ref[r0:r0 + CONV_SUB, :] = (y * _sigmoid(y)).astype(o_ref.dtype)


def _in_proj_body(x_ref, g_ref, w32_ref, cos_ref, sin_ref, cw_ref, cb_ref, cg_ref, cbeta_ref,
                  wup32_ref, wdown32_ref,
                  ya_ref, bx_ref, bg_ref, q_ref, k_ref, v_ref, qp_ref, kp_ref, vp_ref,
                  wup_ref, wdown_ref,
                  w_ref, ubuf, ush, stage_q, stage_k, stage_v, *, splits, n_down_pieces):
    tm = x_ref.shape[0]
    _cast_weights_once(2, [(w32_ref, w_ref)])
    step = pl.program_id(0) * pl.num_programs(1) + pl.program_id(1)
    wup_ref[...] = wup32_ref[...].astype(BF16)

    @pl.when(step < n_down_pieces)
    def _():
        wdown_ref[...] = wdown32_ref[...].astype(BF16)

    _conv_mixer_halo(ubuf, tm)
    hb = _rms(x_ref[...], g_ref[...]).astype(BF16)

    def proj(lo, hi):
        return jnp.dot(hb, w_ref[:, lo:hi], preferred_element_type=F32)

    c_a, c_bx, c_bg, c_q, c_k, c_v = splits
    rows = pl.ds(pl.multiple_of(pl.program_id(1) * tm, tm), tm)
    cos = cos_ref[rows, :]
    sin = sin_ref[rows, :]
    lane = lax.broadcasted_iota(jnp.int32, cos.shape, 1)
    first_half = (lane % HEAD_DIM) < (HEAD_DIM // 2)
    _conv_mixer_prepare(proj(0, c_a), ubuf, ush, ch=c_a // 2)
    for r0 in range(0, tm, CONV_SUB):
        _conv_mixer_rows(r0, cw_ref, cb_ref, cg_ref, cbeta_ref, ya_ref, ubuf, ush)
    bx_ref[...] = proj(c_a, c_bx)
    bg_ref[...] = proj(c_bx, c_bg)
    q = _rope(proj(c_bg, c_q), cos, sin, first_half) * (HEAD_DIM ** -0.5 * LOG2E)
    _store_attn_operand(q, q_ref, qp_ref, stage_q)
    _store_attn_operand(_rope(proj(c_q, c_k), cos, sin, first_half), k_ref, kp_ref, stage_k)
    _store_attn_operand(proj(c_k, c_v), v_ref, vp_ref, stage_v)


def _in_proj(l, x, g, w, cos, sin, conv_w, conv_b, conv_g, conv_beta, w_up, w_down,
             conv_ch, lru_w, att_w):
    B, S, D = x.shape
    tm = ROW_TILE
    n_steps = B * (S // tm)
    bf16_rows = 2 * SUBLANES
    up_rows = w_up.shape[1] // n_steps
    down_rows = LANES
    n_down = w_down.shape[1] // down_rows
    assert up_rows * n_steps == w_up.shape[1] and up_rows % bf16_rows == 0
    assert n_down * down_rows == w_down.shape[1] and n_down <= n_steps
    steps_per_b = S // tm
    up_piece = lambda b, s: (l, b * steps_per_b + s, 0)
    down_piece = lambda b, s: (l, jnp.minimum(b * steps_per_b + s, n_down - 1), 0)
    splits = (2 * conv_ch, 2 * conv_ch + lru_w, 2 * conv_ch + 2 * lru_w,
              2 * conv_ch + 2 * lru_w + att_w, 2 * conv_ch + 2 * lru_w + 2 * att_w,
              2 * conv_ch + 2 * lru_w + 3 * att_w)
    row = lambda n: pl.BlockSpec((None, tm, n), lambda b, s: (b, s, 0))
    perm = pl.BlockSpec((None, ATT_PERM, tm // ATT_PERM, att_w), lambda b, s: (b, 0, s, 0))
    nat_shape = jax.ShapeDtypeStruct((B, S, att_w), BF16)
    perm_shape = jax.ShapeDtypeStruct((B, ATT_PERM, S // ATT_PERM, att_w), BF16)
    stage = pltpu.VMEM((att_w // LANES, tm, LANES), F32)
    return pl.pallas_call(
        functools.partial(_in_proj_body, splits=splits, n_down_pieces=n_down),
        grid=(B, S // tm),
        in_specs=[row(D), _layer(g, l), _layer(w, l), _full(cos.shape), _full(sin.shape),
                  _layer(conv_w, l), _layer(conv_b, l), _layer(conv_g, l), _layer(conv_beta, l),
                  pl.BlockSpec((None, up_rows, w_up.shape[2]), up_piece),
                  pl.BlockSpec((None, down_rows, w_down.shape[2]), down_piece)],
        out_specs=[row(conv_ch), row(lru_w), row(lru_w), row(att_w), row(att_w), row(att_w),
                   perm, perm, perm,
                   pl.BlockSpec((None, up_rows, w_up.shape[2]), lambda b, s: (0,) + up_piece(b, s)[1:]),
                   pl.BlockSpec((None, down_rows, w_down.shape[2]),
                                lambda b, s: (0,) + down_piece(b, s)[1:])],
        out_shape=[jax.ShapeDtypeStruct((B, S, conv_ch), BF16),
                   jax.ShapeDtypeStruct((B, S, lru_w), F32),
                   jax.ShapeDtypeStruct((B, S, lru_w), F32),
                   nat_shape, nat_shape, nat_shape, perm_shape, perm_shape, perm_shape,
                   jax.ShapeDtypeStruct((1,) + w_up.shape[1:], BF16),
                   jax.ShapeDtypeStruct((1,) + w_down.shape[1:], BF16)],
        scratch_shapes=[_mxu_copy(w),
                        pltpu.VMEM((CONV_HALO + tm, conv_ch), F32),
                        pltpu.VMEM((SUBLANES - 1, CONV_HALO + tm - SUBLANES, conv_ch), F32),
                        stage, stage, stage],
        compiler_params=_params("arbitrary", "arbitrary"),
        name="in_proj",
    )(x, g, w, cos, sin, conv_w, conv_b, conv_g, conv_beta, w_up, w_down)


def _lru_body(bx_ref, bg_ref, cw_ref, cb_ref, wa_ref, ba_ref, wi_ref, bi_ref, lam_ref, o_ref,
              xpad, a_s, b_s, *, pitch):
    S, C = bx_ref.shape
    n_slab = C // LANES
    chunk = S // SCAN_CHUNKS
    pad = SUBLANES
    xpad[0:pad, :] = jnp.zeros((pad, C), F32)
    xpad[pad:pad + S, :] = bx_ref[...]
    z = -lam_ref[...]
    log_a_scale = (-LRU_C) * (jnp.maximum(z, 0.0) + jnp.log1p(jnp.exp(-jnp.abs(z))))
    cw = cw_ref[...]
    for j in range(SCAN_CHUNKS):
        base = j * chunk
        u = jnp.zeros((chunk, C), F32) + cb_ref[...]
        for t in range(LRU_CONV_KERNEL):
            lo = base + pad - (LRU_CONV_KERNEL - 1) + t
            u = u + cw[t:t + 1, :] * xpad[lo:lo + chunk, :]
        ub = u.astype(BF16)
        r = _sigmoid(jnp.dot(ub, wa_ref[...], preferred_element_type=F32) + ba_ref[...])
        gate_i = _sigmoid(jnp.dot(ub, wi_ref[...], preferred_element_type=F32) + bi_ref[...])
        log_a = r * log_a_scale
        a = jnp.exp(log_a)
        bterm = jnp.sqrt(jnp.tanh(-log_a) * (1.0 + a * a)) * (gate_i * u)
        for s in range(n_slab):
            a_s[s, j * pitch:j * pitch + chunk, :] = a[:, s * LANES:(s + 1) * LANES]
            b_s[s, j * pitch:j * pitch + chunk, :] = bterm[:, s * LANES:(s + 1) * LANES]

    def step(i, carry):
        out = []
        for s in range(n_slab):
            h, p = carry[2 * s], carry[2 * s + 1]
            idx = pl.ds(i, SCAN_CHUNKS, stride=pitch)
            av = a_s[s, idx, :]
            h = av * h + b_s[s, idx, :]
            p = av * p
            b_s[s, idx, :] = h
            a_s[s, idx, :] = p
            out += [h, p]
        return tuple(out)

    init = []
    for s in range(n_slab):
        init += [jnp.zeros((SCAN_CHUNKS, LANES), F32), jnp.ones((SCAN_CHUNKS, LANES), F32)]
    fin = lax.fori_loop(0, chunk, step, tuple(init), unroll=8)

    for s in range(n_slab):
        h_end, p_end = fin[2 * s], fin[2 * s + 1]
        carry_in = jnp.zeros((1, LANES), F32)
        for j in range(SCAN_CHUNKS):
            rows = slice(j * pitch, j * pitch + chunk)
            h = b_s[s, rows, :] + a_s[s, rows, :] * carry_in
            gate = _gelu_tanh(bg_ref[j * chunk:(j + 1) * chunk, s * LANES:(s + 1) * LANES])
            o_ref[j * chunk:(j + 1) * chunk, s * LANES:(s + 1) * LANES] = (h * gate).astype(o_ref.dtype)
            carry_in = h_end[j:j + 1, :] + p_end[j:j + 1, :] * carry_in


def _lru_mixer(l, bx, bg, cw, cb, wa, ba, wi, bi, lam):
    B, S, C = bx.shape
    chunk = S // SCAN_CHUNKS
    pitch = chunk + SUBLANES
    seq = pl.BlockSpec((None, S, C), lambda b: (b, 0, 0))
    params = (cw, cb, wa, ba, wi, bi, lam)
    return pl.pallas_call(
        functools.partial(_lru_body, pitch=pitch),
        grid=(B,),
        in_specs=[seq, seq] + [_layer(p, l) for p in params],
        out_specs=seq,
        out_shape=jax.ShapeDtypeStruct((B, S, C), BF16),
        scratch_shapes=[pltpu.VMEM((S + SUBLANES, C), F32),
                        pltpu.VMEM((C // LANES, SCAN_CHUNKS * pitch, LANES), F32),
                        pltpu.VMEM((C // LANES, SCAN_CHUNKS * pitch, LANES), F32)],
        compiler_params=_params("arbitrary"),
        name="lru_mixer",
    )(bx, bg, *params)


def _attn_bias_tables(seq):
    blk = ATT_BLOCK
    local = [(w, d) for w, d in DILATED_PATTERNS if d == 1]
    strided = [(w, d) for w, d in DILATED_PATTERNS if d != 1]
    for window, dil in DILATED_PATTERNS:
        n = window // dil
        assert seq % dil == 0 and (seq // dil) % n == 0, "pattern padding not supported"
    assert all(w <= blk for w, _ in local) and all(d % ATT_PERM == 0 for _, d in strided)
    assert seq % (ATT_PERM * blk) == 0

    def bias(mult):
        return jnp.asarray(np.where(mult > 0, np.log2(np.maximum(mult, 1)), NEG_BIAS), F32)

    r = np.arange(blk)[:, None]
    d = r - np.arange(2 * blk)[None, :] + blk
    near = sum(((d >= 0) & (d <= w)).astype(np.int64) for w, _ in local)
    n_sub = seq // ATT_PERM // blk
    d = ATT_PERM * (r - np.arange(n_sub * blk)[None, :] + (n_sub - 1) * blk)
    far = sum(((d >= 0) & (d % dil == 0) & (d <= w)).astype(np.int64) for w, dil in strided)
    return bias(near), bias(far)


def _scores(q, k):
    return lax.dot_general(q, k, (((1,), (1,)), ((), ())), preferred_element_type=F32)


def _attn_body(q_ref, k_ref, v_ref, qp_ref, kp_ref, vp_ref, near_ref, far_ref, o_ref,
               vn, vp, acc_far, max_far):
    S = q_ref.shape[0]
    blk = ATT_BLOCK
    n_res, sub_len = qp_ref.shape[0], qp_ref.shape[1]
    n_sub = sub_len // blk
    head0 = lax.broadcasted_iota(jnp.int32, (blk, LANES), 1) < HEAD_DIM
    head0_sub = lax.broadcasted_iota(jnp.int32, (sub_len, LANES), 1) < HEAD_DIM
    one = jnp.ones((sub_len, LANES), BF16)
    for h in range(2):
        own = head0_sub if h == 0 else jnp.logical_not(head0_sub)
        for r in range(n_res):
            vp[h, r] = jnp.where(own, vp_ref[r], one)
        for c in range(S // sub_len):
            rows = slice(c * sub_len, (c + 1) * sub_len)
            vn[h, rows, :] = jnp.where(own, v_ref[rows, :], one)

    def own_lanes(h):
        return head0 if h == 0 else jnp.logical_not(head0)

    def far_block(h, r, a):
        nk = (a + 1) * blk

        def scores():
            q = qp_ref[r, a * blk:(a + 1) * blk, :]
            q = jnp.where(own_lanes(h), q, jnp.zeros_like(q))
            return _scores(q, kp_ref[r, 0:nk, :]) + far_ref[:, (n_sub - 1 - a) * blk:]

        def finish(s):
            m = jnp.max(s, axis=-1, keepdims=True)
            p = jnp.exp2(s - m).astype(BF16)
            rows = pl.ds(r + n_res * a * blk, blk, stride=n_res)
            acc_far[h, rows, :] = jnp.dot(p, vp[h, r, 0:nk, :], preferred_element_type=F32)
            max_far[h, rows, :] = jnp.broadcast_to(m, (blk, LANES))

        return scores, finish

    def near_block(i):
        rows = slice(i * blk, (i + 1) * blk)
        keys = slice(max(i - 1, 0) * blk, (i + 1) * blk)
        bias = near_ref[...] if i > 0 else near_ref[:, blk:]

        def scores():
            q = q_ref[rows, :]
            return [_scores(jnp.where(own_lanes(h), q, jnp.zeros_like(q)), k_ref[keys, :]) + bias
                    for h in range(2)]

        def finish(s_heads):
            acc = []
            for h, s in enumerate(s_heads):
                m_far = max_far[h, rows, :]
                m = jnp.maximum(jnp.max(s, axis=-1, keepdims=True), m_far)
                m_keys = m if i == 0 else jnp.concatenate([m, m], axis=-1)
                p = jnp.exp2(s - m_keys).astype(BF16)
                acc.append(jnp.dot(p, vn[h, keys, :], preferred_element_type=F32)
                           + jnp.exp2(m_far - m) * acc_far[h, rows, :])
            num = jnp.where(head0, acc[0], acc[1])
            den = pltpu.roll(jnp.where(head0, acc[1], acc[0]), HEAD_DIM, axis=1)
            o_ref[rows, :] = (num / den).astype(o_ref.dtype)

        return scores, finish

    blocks = []
    for a in range(n_sub):
        blocks += [far_block(h, r, a) for r in range(n_res) for h in range(2)]
        blocks += [near_block(i) for i in range(a * n_res, (a + 1) * n_res)]
    pending = []
    for scores, finish in blocks:
        pending.append((finish, scores()))
        if len(pending) > ATT_PIPELINE:
            fn, s = pending.pop(0)
            fn(s)
    for fn, s in pending:
        fn(s)


def _dilated_attention(q, k, v, qp, kp, vp, near, far):
    B, S, W = q.shape
    n_res, sub_len = qp.shape[1], qp.shape[2]
    nat = pl.BlockSpec((None, S, LANES), lambda b, h: (b, 0, h))
    perm = pl.BlockSpec((None, n_res, sub_len, LANES), lambda b, h: (b, 0, 0, h))
    return pl.pallas_call(
        _attn_body,
        grid=(B, W // LANES),
        in_specs=[nat, nat, nat, perm, perm, perm, _full(near.shape), _full(far.shape)],
        out_specs=nat,
        out_shape=jax.ShapeDtypeStruct((B, S, W), BF16),
        scratch_shapes=[pltpu.VMEM((2, S, LANES), BF16),
                        pltpu.VMEM((2, n_res, sub_len, LANES), BF16),
                        pltpu.VMEM((2, S, LANES), F32),
                        pltpu.VMEM((2, S, LANES), F32)],
        compiler_params=_params("arbitrary", "arbitrary"),
        name="dilated_attn",
    )(q, k, v, qp, kp, vp, near, far)


def _norm_matmul_body(x_ref, g_ref, w32_ref, o_ref, w_ref):
    _cast_weights_once(1, [(w32_ref, w_ref)])
    hb = _rms(x_ref[...], g_ref[...]).astype(BF16)
    o_ref[...] = jnp.dot(hb, w_ref[...], preferred_element_type=F32).astype(o_ref.dtype)


def _mem_kv_proj(l, x, g, w):
    T, D = x.shape
    N = w.shape[-1]
    tm = ROW_TILE
    return pl.pallas_call(
        _norm_matmul_body,
        grid=(T // tm,),
        in_specs=[pl.BlockSpec((tm, D), lambda i: (i, 0)), _layer(g, l), _layer(w, l)],
        out_specs=pl.BlockSpec((tm, N), lambda i: (i, 0)),
        out_shape=jax.ShapeDtypeStruct((T, N), BF16),
        scratch_shapes=[_mxu_copy(w)],
        compiler_params=_params("arbitrary"),
        name="mem_kv_proj",
    )(x, g, w)


def _mix_out_mem_body(ya_ref, yb_ref, yc_ref, x_ref, wout32_ref, gmix_ref, gpre_ref, wq32_ref, kv_ref,
                      wo32_ref, gpost_ref, o_ref, wout_ref, wq_ref, wo_ref):
    _cast_weights_once(2, [(wout32_ref, wout_ref), (wq32_ref, wq_ref), (wo32_ref, wo_ref)])
    tm, D = x_ref.shape
    hd = D // MEM_HEADS

    def stages(rows):
        y = jnp.concatenate([ya_ref[rows, :], yb_ref[rows, :], yc_ref[rows, :]], axis=-1)
        y = jnp.dot(y, wout_ref[...], preferred_element_type=F32)
        yield
        x = x_ref[rows, :] + _rms(y, gmix_ref[...])
        hb = _rms(x, gpre_ref[...]).astype(BF16)
        yield
        q = jnp.dot(hb, wq_ref[...], preferred_element_type=F32) * (hd ** -0.5 * LOG2E)
        q = q.astype(BF16)
        scores = [_scores(q[:, h * hd:(h + 1) * hd], kv_ref[:, h * hd:(h + 1) * hd])
                  for h in range(MEM_HEADS)]
        yield
        probs = []
        for s in scores:
            p = jnp.exp2(s - jnp.max(s, axis=-1, keepdims=True))
            probs.append((p.astype(BF16), jnp.sum(p, axis=-1, keepdims=True)))
        yield
        outs = []
        for h, (p, den) in enumerate(probs):
            vh = kv_ref[:, D + h * hd:D + (h + 1) * hd]
            outs.append((jnp.dot(p, vh, preferred_element_type=F32) / den).astype(BF16))
        y = jnp.dot(jnp.concatenate(outs, axis=-1), wo_ref[...], preferred_element_type=F32)
        yield
        o_ref[rows, :] = x + _rms(y, gpost_ref[...])

    half = tm // MIX_ROW_GROUPS
    chains = [stages(slice(g * half, (g + 1) * half)) for g in range(MIX_ROW_GROUPS)]
    while chains:
        chains = [c for c in chains if next(c, StopIteration) is not StopIteration]


def _mix_out_mem(l, ya, yb, yc, x, wout, gmix, gpre, wq, kv, wo, gpost):
    B, S, D = x.shape
    M = kv.shape[1]
    tm = ROW_TILE
    row = lambda n: pl.BlockSpec((None, tm, n), lambda b, s: (b, s, 0))
    return pl.pallas_call(
        _mix_out_mem_body,
        grid=(B, S // tm),
        in_specs=[row(ya.shape[-1]), row(yb.shape[-1]), row(yc.shape[-1]), row(D),
                  _layer(wout, l), _layer(gmix, l), _layer(gpre, l), _layer(wq, l),
                  pl.BlockSpec((None, M, 2 * D), lambda b, s: (b, 0, 0)),
                  _layer(wo, l), _layer(gpost, l)],
        out_specs=row(D),
        out_shape=jax.ShapeDtypeStruct((B, S, D), F32),
        scratch_shapes=[_mxu_copy(wout), _mxu_copy(wq), _mxu_copy(wo)],
        compiler_params=_params("arbitrary", "arbitrary"),
        name="mix_out_mem_attn",
    )(ya, yb, yc, x, wout, gmix, gpre, wq, kv, wo, gpost)


def _ffn_body(x_ref, gpre_ref, wup_ref, cw_ref, cb_ref, wd_ref, gpost_ref, o_ref,
              carry, act, *, d_ff, tf):
    tm = x_ref.shape[0]
    pad = SUBLANES
    n_chunks = d_ff // tf
    first_rows = lax.broadcasted_iota(jnp.int32, (pad, tf), 0)

    @pl.when(pl.program_id(1) == 0)
    def _():
        carry[...] = jnp.zeros(carry.shape, F32)

    x = x_ref[...]
    hb = _rms(x, gpre_ref[...]).astype(BF16)

    def conv(col0):
        cols = slice(col0, col0 + tf)
        u = jnp.dot(hb, wup_ref[:, cols], preferred_element_type=F32)
        last = carry[:, cols]
        carry[:, cols] = u[tm - pad:tm, :]
        w = cw_ref[:, cols]
        y = cb_ref[:, cols] + w[FFN_CONV_KERNEL - 1:FFN_CONV_KERNEL, :] * u
        for t in range(FFN_CONV_KERNEL - 1):
            back = FFN_CONV_KERNEL - 1 - t
            shifted = pltpu.roll(u, back, axis=0)
            head = jnp.where(first_rows < back, pltpu.roll(last, back, axis=0), shifted[0:pad, :])
            y = y + w[t:t + 1, :] * jnp.concatenate([head, shifted[pad:, :]], axis=0)
        return y

    for c in range(n_chunks):
        gate = conv(c * tf)
        up = conv(d_ff + c * tf)
        act[:, c * tf:(c + 1) * tf] = (_gelu_tanh(gate) * up).astype(BF16)
    y = jnp.dot(act[...], wd_ref[...], preferred_element_type=F32)
    o_ref[...] = x + _rms(y, gpost_ref[...])


def _ffn(l, x, gpre, wup, cw, cb, wd, gpost):
    B, S, D = x.shape
    d_ff = wd.shape[1]
    tm = ROW_TILE
    tf = MXU_DIM
    row = pl.BlockSpec((None, tm, D), lambda b, s: (b, s, 0))
    return pl.pallas_call(
        functools.partial(_ffn_body, d_ff=d_ff, tf=tf),
        grid=(B, S // tm),
        in_specs=[row, _layer(gpre, l), _layer(wup, 0), _layer(cw, l), _layer(cb, l),
                  _layer(wd, 0), _layer(gpost, l)],
        out_specs=row,
        out_shape=jax.ShapeDtypeStruct((B, S, D), F32),
        scratch_shapes=[pltpu.VMEM((SUBLANES, 2 * d_ff), F32),
                        pltpu.VMEM((tm, d_ff), BF16)],
        compiler_params=_params("arbitrary", "arbitrary"),
        name="conv_ffn",
    )(x, gpre, wup, cw, cb, wd, gpost)


def _rope_tables(seq, heads):
    inv = 1.0 / (ROPE_THETA ** (jnp.arange(0, HEAD_DIM, 2, dtype=F32) / HEAD_DIM))
    ang = jnp.arange(seq, dtype=F32)[:, None] * inv[None, :]
    cos, sin = jnp.cos(ang), jnp.sin(ang)
    cos_h = jnp.concatenate([cos, cos], axis=-1)
    sin_h = jnp.concatenate([-sin, sin], axis=-1)
    return jnp.tile(cos_h, (1, heads)), jnp.tile(sin_h, (1, heads))


def _block_diag(w):
    depth, h, n, _ = w.shape
    eye = jnp.eye(h, dtype=w.dtype)
    return jnp.einsum('lhij,hg->lhigj', w, eye).reshape(depth, h * n, h * n)


def kernel(x, mem, g_mix_pre, w_in, conv_w, conv_b, conv_ln_g, conv_ln_b, lru_conv_w, lru_conv_b, lru_w_a, lru_b_a, lru_w_i, lru_b_i, lru_lambda, w_out, g_mix_post, g_mem_pre, g_mem_kv, w_mem_q, w_mem_kv, w_mem_o, g_mem_post, g_ffn_pre, w_up, ffn_conv_w, ffn_conv_b, w_down, g_ffn_post):
    B, S, D = x.shape
    depth = w_in.shape[0]
    conv_ch = conv_w.shape[-1]
    lru_w = lru_conv_w.shape[-1]
    att_w = w_out.shape[1] - conv_ch - lru_w
    n_mem = mem.shape[1]
    assert S % ROW_TILE == 0 and ROW_TILE % (ATT_PERM * SUBLANES * 2) == 0
    assert att_w % LANES == 0 and LANES == 2 * HEAD_DIM and (B * n_mem) % ROW_TILE == 0

    cos, sin = _rope_tables(S, att_w // HEAD_DIM)
    near, far = _attn_bias_tables(S)
    vec = lambda p: p.reshape(depth, 1, -1)
    mxu = lambda p: p.astype(BF16)
    mem2 = mem.reshape(B * n_mem, D)

    g_mix_pre, g_mix_post, g_mem_pre, g_mem_kv, g_mem_post, g_ffn_pre, g_ffn_post = map(
        vec, (g_mix_pre, g_mix_post, g_mem_pre, g_mem_kv, g_mem_post, g_ffn_pre, g_ffn_post))
    conv_b, conv_ln_g, conv_ln_b, lru_conv_b, lru_b_a, lru_b_i, lru_lambda, ffn_conv_b = map(
        vec, (conv_b, conv_ln_g, conv_ln_b, lru_conv_b, lru_b_a, lru_b_i, lru_lambda, ffn_conv_b))
    lru_w_a = mxu(_block_diag(lru_w_a))
    lru_w_i = mxu(_block_diag(lru_w_i))

    for l in range(depth):
        y_a, bx, bg, q, k, v, qp, kp, vp, w_up_l, w_down_l = _in_proj(
            l, x, g_mix_pre, w_in, cos, sin, conv_w, conv_b, conv_ln_g, conv_ln_b, w_up, w_down,
            conv_ch, lru_w, att_w)
        y_b = _lru_mixer(l, bx, bg, lru_conv_w, lru_conv_b, lru_w_a, lru_b_a, lru_w_i, lru_b_i,
                         lru_lambda)
        y_c = _dilated_attention(q, k, v, qp, kp, vp, near, far)
        kv = _mem_kv_proj(l, mem2, g_mem_kv, w_mem_kv).reshape(B, n_mem, 2 * D)
        x = _mix_out_mem(l, y_a, y_b, y_c, x, w_out, g_mix_post, g_mem_pre, w_mem_q, kv, w_mem_o,
                         g_mem_post)
        x = _ffn(l, x, g_ffn_pre, w_up_l, ffn_conv_w, ffn_conv_b, w_down_l, g_ffn_post)
    return x
```

```python
import functools
import math

import numpy as np
import jax
import jax.numpy as jnp
from jax import lax
from jax.experimental import pallas as pl
from jax.experimental.pallas import tpu as pltpu

F32 = jnp.float32
BF16 = jnp.bfloat16

HEAD_DIM = 64
CONV_KERNEL = 31
LRU_CONV_KERNEL = 4
LRU_C = 8.0
DILATED_PATTERNS = ((128, 1), (512, 4), (2048, 16))
ROPE_THETA = 10000.0
MEM_HEADS = 4
FFN_CONV_KERNEL = 3
EPS = 1e-6

LANES = 128
SUBLANES = 8
MXU_DIM = 256
VMEM_LIMIT_BYTES = 56 * 1024 * 1024

ROW_TILE = 512
ATT_BLOCK = 128
ATT_PERM = 4
ATT_PIPELINE = 4
CONV_HALO = 32
CONV_SUB = 64
MIX_ROW_GROUPS = 2
LOG2E = 1.4426950408889634
SCAN_CHUNKS = SUBLANES
NEG_BIAS = -1e30


def _params(*sem):
    return pltpu.CompilerParams(dimension_semantics=sem, vmem_limit_bytes=VMEM_LIMIT_BYTES)


def _rms(x, g):
    return x * lax.rsqrt(jnp.mean(x * x, axis=-1, keepdims=True) + EPS) * g


def _sigmoid(x):
    return 1.0 / (1.0 + jnp.exp2(x * (-LOG2E)))


def _gelu_tanh(x):
    k = -2.0 * math.sqrt(2.0 / math.pi) * LOG2E
    return x / (1.0 + jnp.exp2(x * (k + (0.044715 * k) * (x * x))))


def _full(shape):
    n = len(shape)
    return pl.BlockSpec(shape, lambda *_: (0,) * n)


def _layer(p, l):
    return pl.BlockSpec((None,) + p.shape[1:], lambda *_: (l, 0, 0), pipeline_mode=pl.Buffered(1))


def _cast_weights_once(grid_rank, pairs):
    first = pl.program_id(0) == 0
    for axis in range(1, grid_rank):
        first = jnp.logical_and(first, pl.program_id(axis) == 0)

    @pl.when(first)
    def _():
        for src, dst in pairs:
            dst[...] = src[...].astype(dst.dtype)


def _mxu_copy(w):
    return pltpu.VMEM(w.shape[1:], BF16)


def _rope(t, cos, sin_signed, first_half):
    n = t.shape[-1]
    half = HEAD_DIM // 2
    partner = jnp.where(first_half, pltpu.roll(t, n - half, axis=1), pltpu.roll(t, half, axis=1))
    return t * cos + partner * sin_signed


def _store_attn_operand(t, nat_ref, perm_ref, stage):
    rows, width = t.shape
    nat_ref[...] = t.astype(BF16)
    for c in range(width // LANES):
        stage[c] = t[:, c * LANES:(c + 1) * LANES]
    for c in range(width // LANES):
        for r in range(ATT_PERM):
            sub = stage[c, pl.ds(r, rows // ATT_PERM, stride=ATT_PERM), :]
            perm_ref[r, :, c * LANES:(c + 1) * LANES] = sub.astype(BF16)


def _conv_mixer_halo(ubuf, tm):
    @pl.when(pl.program_id(1) == 0)
    def _():
        ubuf[0:CONV_HALO, :] = jnp.zeros((CONV_HALO, ubuf.shape[-1]), F32)

    @pl.when(pl.program_id(1) > 0)
    def _():
        ubuf[0:CONV_HALO, :] = ubuf[tm:tm + CONV_HALO, :]


def _conv_mixer_prepare(a, ubuf, ush, *, ch):
    tm = a.shape[0]
    ubuf[CONV_HALO:CONV_HALO + tm, :] = a[:, :ch] * _sigmoid(a[:, ch:])
    span = CONV_HALO + tm - SUBLANES
    for s in range(1, SUBLANES):
        ush[s - 1, 0:span, :] = ubuf[s:s + span, :]


def _conv_mixer_rows(r0, w_ref, b_ref, g_ref, beta_ref, o_ref, ubuf, ush):
    ch = ubuf.shape[-1]
    w = w_ref[...]
    off = CONV_HALO - (CONV_KERNEL - 1)
    acc = jnp.zeros((CONV_SUB, ch), F32) + b_ref[...]
    for j in range(CONV_KERNEL):
        s = (off + j) % SUBLANES
        lo = r0 + off + j - s
        win = ubuf[lo:lo + CONV_SUB, :] if s == 0 else ush[s - 1, lo:lo + CONV_SUB, :]
        acc = acc + w[j:j + 1, :] * win
    mu = jnp.mean(acc, axis=-1, keepdims=True)
    cen = acc - mu
    var = jnp.mean(cen * cen, axis=-1, keepdims=True)
    y = cen * lax.rsqrt(var + EPS) * g_ref[...] + beta_ref[...]
    o_ref[r0:r0 + CONV_SUB, :] = (y * _sigmoid(y)).astype(o_ref.dtype)


def _in_proj_body(x_ref, g_ref, w32_ref, cos_ref, sin_ref, cw_ref, cb_ref, cg_ref, cbeta_ref,
                  wup32_ref, wdown32_ref,
                  ya_ref, bx_ref, bg_ref, q_ref, k_ref, v_ref, qp_ref, kp_ref, vp_ref,
                  wup_ref, wdown_ref,
                  w_ref, ubuf, ush, stage_q, stage_k, stage_v, *, splits, n_down_pieces):
    tm = x_ref.shape[0]
    _cast_weights_once(2, [(w32_ref, w_ref)])
    step = pl.program_id(0) * pl.num_programs(1) + pl.program_id(1)
    wup_ref[...] = wup32_ref[...].astype(BF16)

    @pl.when(step < n_down_pieces)
    def _():
        wdown_ref[...] = wdown32_ref[...].astype(BF16)

    _conv_mixer_halo(ubuf, tm)
    hb = _rms(x_ref[...], g_ref[...]).astype(BF16)

    def proj(lo, hi):
        return jnp.dot(hb, w_ref[:, lo:hi], preferred_element_type=F32)

    c_a, c_bx, c_bg, c_q, c_k, c_v = splits
    rows = pl.ds(pl.multiple_of(pl.program_id(1) * tm, tm), tm)
    cos = cos_ref[rows, :]
    sin = sin_ref[rows, :]
    lane = lax.broadcasted_iota(jnp.int32, cos.shape, 1)
    first_half = (lane % HEAD_DIM) < (HEAD_DIM // 2)
    _conv_mixer_prepare(proj(0, c_a), ubuf, ush, ch=c_a // 2)
    for r0 in range(0, tm, CONV_SUB):
        _conv_mixer_rows(r0, cw_ref, cb_ref, cg_ref, cbeta_ref, ya_ref, ubuf, ush)
    bx_ref[...] = proj(c_a, c_bx)
    bg_ref[...] = proj(c_bx, c_bg)
    q = _rope(proj(c_bg, c_q), cos, sin, first_half) * (HEAD_DIM ** -0.5 * LOG2E)
    _store_attn_operand(q, q_ref, qp_ref, stage_q)
    _store_attn_operand(_rope(proj(c_q, c_k), cos, sin, first_half), k_ref, kp_ref, stage_k)
    _store_attn_operand(proj(c_k, c_v), v_ref, vp_ref, stage_v)


def _in_proj(l, x, g, w, cos, sin, conv_w, conv_b, conv_g, conv_beta, w_up, w_down,
             conv_ch, lru_w, att_w):
    B, S, D = x.shape
    tm = ROW_TILE
    n_steps = B * (S // tm)
    bf16_rows = 2 * SUBLANES
    up_rows = w_up.shape[1] // n_steps
    down_rows = LANES
    n_down = w_down.shape[1] // down_rows
    assert up_rows * n_steps == w_up.shape[1] and up_rows % bf16_rows == 0
    assert n_down * down_rows == w_down.shape[1] and n_down <= n_steps
    steps_per_b = S // tm
    up_piece = lambda b, s: (l, b * steps_per_b + s, 0)
    down_piece = lambda b, s: (l, jnp.minimum(b * steps_per_b + s, n_down - 1), 0)
    splits = (2 * conv_ch, 2 * conv_ch + lru_w, 2 * conv_ch + 2 * lru_w,
              2 * conv_ch + 2 * lru_w + att_w, 2 * conv_ch + 2 * lru_w + 2 * att_w,
              2 * conv_ch + 2 * lru_w + 3 * att_w)
    row = lambda n: pl.BlockSpec((None, tm, n), lambda b, s: (b, s, 0))
    perm = pl.BlockSpec((None, ATT_PERM, tm // ATT_PERM, att_w), lambda b, s: (b, 0, s, 0))
    nat_shape = jax.ShapeDtypeStruct((B, S, att_w), BF16)
    perm_shape = jax.ShapeDtypeStruct((B, ATT_PERM, S // ATT_PERM, att_w), BF16)
    stage = pltpu.VMEM((att_w // LANES, tm, LANES), F32)
    return pl.pallas_call(
        functools.partial(_in_proj_body, splits=splits, n_down_pieces=n_down),
        grid=(B, S // tm),
        in_specs=[row(D), _layer(g, l), _layer(w, l), _full(cos.shape), _full(sin.shape),
                  _layer(conv_w, l), _layer(conv_b, l), _layer(conv_g, l), _layer(conv_beta, l),
                  pl.BlockSpec((None, up_rows, w_up.shape[2]), up_piece),
                  pl.BlockSpec((None, down_rows, w_down.shape[2]), down_piece)],
        out_specs=[row(conv_ch), row(lru_w), row(lru_w), row(att_w), row(att_w), row(att_w),
                   perm, perm, perm,
                   pl.BlockSpec((None, up_rows, w_up.shape[2]), lambda b, s: (0,) + up_piece(b, s)[1:]),
                   pl.BlockSpec((None, down_rows, w_down.shape[2]),
                                lambda b, s: (0,) + down_piece(b, s)[1:])],
        out_shape=[jax.ShapeDtypeStruct((B, S, conv_ch), BF16),
                   jax.ShapeDtypeStruct((B, S, lru_w), F32),
                   jax.ShapeDtypeStruct((B, S, lru_w), F32),
                   nat_shape, nat_shape, nat_shape, perm_shape, perm_shape, perm_shape,
                   jax.ShapeDtypeStruct((1,) + w_up.shape[1:], BF16),
                   jax.ShapeDtypeStruct((1,) + w_down.shape[1:], BF16)],
        scratch_shapes=[_mxu_copy(w),
                        pltpu.VMEM((CONV_HALO + tm, conv_ch), F32),
                        pltpu.VMEM((SUBLANES - 1, CONV_HALO + tm - SUBLANES, conv_ch), F32),
                        stage, stage, stage],
        compiler_params=_params("arbitrary", "arbitrary"),
        name="in_proj",
    )(x, g, w, cos, sin, conv_w, conv_b, conv_g, conv_beta, w_up, w_down)


def _lru_body(bx_ref, bg_ref, cw_ref, cb_ref, wa_ref, ba_ref, wi_ref, bi_ref, lam_ref, o_ref,
              a_s, b_s, *, pitch):
    S, C = bx_ref.shape
    n_slab = C // LANES
    chunk = S // SCAN_CHUNKS
    pad = SUBLANES
    first_rows = lax.broadcasted_iota(jnp.int32, (pad, C), 0)
    z = -lam_ref[...]
    log_a_scale = (-LRU_C) * (jnp.maximum(z, 0.0) + jnp.log1p(jnp.exp(-jnp.abs(z))))
    cw = cw_ref[...]
    for j in range(SCAN_CHUNKS):
        base = j * chunk
        xc = bx_ref[base:base + chunk, :]
        last = bx_ref[base - pad:base, :] if j > 0 else jnp.zeros((pad, C), F32)
        u = cb_ref[...] + cw[LRU_CONV_KERNEL - 1:LRU_CONV_KERNEL, :] * xc
        for t in range(LRU_CONV_KERNEL - 1):
            back = LRU_CONV_KERNEL - 1 - t
            shifted = pltpu.roll(xc, back, axis=0)
            head = jnp.where(first_rows < back, pltpu.roll(last, back, axis=0), shifted[0:pad, :])
            u = u + cw[t:t + 1, :] * jnp.concatenate([head, shifted[pad:, :]], axis=0)
        ub = u.astype(BF16)
        r = _sigmoid(jnp.dot(ub, wa_ref[...], preferred_element_type=F32) + ba_ref[...])
        gate_i = _sigmoid(jnp.dot(ub, wi_ref[...], preferred_element_type=F32) + bi_ref[...])
        log_a = r * log_a_scale
        a = jnp.exp(log_a)
        bterm = jnp.sqrt(jnp.tanh(-log_a) * (1.0 + a * a)) * (gate_i * u)
        for s in range(n_slab):
            a_s[s, j * pitch:j * pitch + chunk, :] = a[:, s * LANES:(s + 1) * LANES]
            b_s[s, j * pitch:j * pitch + chunk, :] = bterm[:, s * LANES:(s + 1) * LANES]

    def step(i, carry):
        out = []
        for s in range(n_slab):
            h, p = carry[2 * s], carry[2 * s + 1]
            idx = pl.ds(i, SCAN_CHUNKS, stride=pitch)
            av = a_s[s, idx, :]
            h = av * h + b_s[s, idx, :]
            p = av * p
            b_s[s, idx, :] = h
            a_s[s, idx, :] = p
            out += [h, p]
        return tuple(out)

    init = []
    for s in range(n_slab):
        init += [jnp.zeros((SCAN_CHUNKS, LANES), F32), jnp.ones((SCAN_CHUNKS, LANES), F32)]
    fin = lax.fori_loop(0, chunk, step, tuple(init), unroll=8)

    for s in range(n_slab):
        h_end, p_end = fin[2 * s], fin[2 * s + 1]
        carry_in = jnp.zeros((1, LANES), F32)
        for j in range(SCAN_CHUNKS):
            rows = slice(j * pitch, j * pitch + chunk)
            h = b_s[s, rows, :] + a_s[s, rows, :] * carry_in
            gate = _gelu_tanh(bg_ref[j * chunk:(j + 1) * chunk, s * LANES:(s + 1) * LANES])
            o_ref[j * chunk:(j + 1) * chunk, s * LANES:(s + 1) * LANES] = (h * gate).astype(o_ref.dtype)
            carry_in = h_end[j:j + 1, :] + p_end[j:j + 1, :] * carry_in


def _lru_mixer(l, bx, bg, cw, cb, wa, ba, wi, bi, lam):
    B, S, C = bx.shape
    chunk = S // SCAN_CHUNKS
    pitch = chunk + SUBLANES
    seq = pl.BlockSpec((None, S, C), lambda b: (b, 0, 0))
    params = (cw, cb, wa, ba, wi, bi, lam)
    return pl.pallas_call(
        functools.partial(_lru_body, pitch=pitch),
        grid=(B,),
        in_specs=[seq, seq] + [_layer(p, l) for p in params],
        out_specs=seq,
        out_shape=jax.ShapeDtypeStruct((B, S, C), BF16),
        scratch_shapes=[pltpu.VMEM((C // LANES, SCAN_CHUNKS * pitch, LANES), F32),
                        pltpu.VMEM((C // LANES, SCAN_CHUNKS * pitch, LANES), F32)],
        compiler_params=_params("arbitrary"),
        name="lru_mixer",
    )(bx, bg, *params)


def _attn_bias_tables(seq):
    blk = ATT_BLOCK
    local = [(w, d) for w, d in DILATED_PATTERNS if d == 1]
    strided = [(w, d) for w, d in DILATED_PATTERNS if d != 1]
    for window, dil in DILATED_PATTERNS:
        n = window // dil
        assert seq % dil == 0 and (seq // dil) % n == 0, "pattern padding not supported"
    assert all(w <= blk for w, _ in local) and all(d % ATT_PERM == 0 for _, d in strided)
    assert seq % (ATT_PERM * blk) == 0

    def bias(mult):
        return jnp.asarray(np.where(mult > 0, np.log2(np.maximum(mult, 1)), NEG_BIAS), F32)

    r = np.arange(blk)[:, None]
    d = r - np.arange(2 * blk)[None, :] + blk
    near = sum(((d >= 0) & (d <= w)).astype(np.int64) for w, _ in local)
    n_sub = seq // ATT_PERM // blk
    d = ATT_PERM * (r - np.arange(n_sub * blk)[None, :] + (n_sub - 1) * blk)
    far = sum(((d >= 0) & (d % dil == 0) & (d <= w)).astype(np.int64) for w, dil in strided)
    return bias(near), bias(far)


def _scores(q, k):
    return lax.dot_general(q, k, (((1,), (1,)), ((), ())), preferred_element_type=F32)


def _attn_body(q_ref, k_ref, v_ref, qp_ref, kp_ref, vp_ref, near_ref, far_ref, o_ref,
               vn, vp, acc_far, max_far):
    S = q_ref.shape[0]
    blk = ATT_BLOCK
    n_res, sub_len = qp_ref.shape[0], qp_ref.shape[1]
    n_sub = sub_len // blk
    head0 = lax.broadcasted_iota(jnp.int32, (blk, LANES), 1) < HEAD_DIM
    head0_sub = lax.broadcasted_iota(jnp.int32, (sub_len, LANES), 1) < HEAD_DIM
    one = jnp.ones((sub_len, LANES), BF16)
    for h in range(2):
        own = head0_sub if h == 0 else jnp.logical_not(head0_sub)
        for r in range(n_res):
            vp[h, r] = jnp.where(own, vp_ref[r], one)
        for c in range(S // sub_len):
            rows = slice(c * sub_len, (c + 1) * sub_len)
            vn[h, rows, :] = jnp.where(own, v_ref[rows, :], one)

    def own_lanes(h):
        return head0 if h == 0 else jnp.logical_not(head0)

    def far_block(h, r, a):
        nk = (a + 1) * blk

        def scores():
            q = qp_ref[r, a * blk:(a + 1) * blk, :]
            q = jnp.where(own_lanes(h), q, jnp.zeros_like(q))
            return _scores(q, kp_ref[r, 0:nk, :]) + far_ref[:, (n_sub - 1 - a) * blk:]

        def finish(s):
            m = jnp.max(s, axis=-1, keepdims=True)
            p = jnp.exp2(s - m).astype(BF16)
            rows = pl.ds(r + n_res * a * blk, blk, stride=n_res)
            acc_far[h, rows, :] = jnp.dot(p, vp[h, r, 0:nk, :], preferred_element_type=F32)
            max_far[h, rows, :] = jnp.broadcast_to(m, (blk, LANES))

        return scores, finish

    def near_block(i):
        rows = slice(i * blk, (i + 1) * blk)
        keys = slice(max(i - 1, 0) * blk, (i + 1) * blk)
        bias = near_ref[...] if i > 0 else near_ref[:, blk:]

        def scores():
            q = q_ref[rows, :]
            return [_scores(jnp.where(own_lanes(h), q, jnp.zeros_like(q)), k_ref[keys, :]) + bias
                    for h in range(2)]

        def finish(s_heads):
            acc = []
            for h, s in enumerate(s_heads):
                m_far = max_far[h, rows, :]
                m = jnp.maximum(jnp.max(s, axis=-1, keepdims=True), m_far)
                m_keys = m if i == 0 else jnp.concatenate([m, m], axis=-1)
                p = jnp.exp2(s - m_keys).astype(BF16)
                acc.append(jnp.dot(p, vn[h, keys, :], preferred_element_type=F32)
                           + jnp.exp2(m_far - m) * acc_far[h, rows, :])
            num = jnp.where(head0, acc[0], acc[1])
            den = pltpu.roll(jnp.where(head0, acc[1], acc[0]), HEAD_DIM, axis=1)
            o_ref[rows, :] = (num / den).astype(o_ref.dtype)

        return scores, finish

    blocks = []
    for a in range(n_sub):
        blocks += [far_block(h, r, a) for r in range(n_res) for h in range(2)]
        blocks += [near_block(i) for i in range(a * n_res, (a + 1) * n_res)]
    pending = []
    for scores, finish in blocks:
        pending.append((finish, scores()))
        if len(pending) > ATT_PIPELINE:
            fn, s = pending.pop(0)
            fn(s)
    for fn, s in pending:
        fn(s)


def _dilated_attention(q, k, v, qp, kp, vp, near, far):
    B, S, W = q.shape
    n_res, sub_len = qp.shape[1], qp.shape[2]
    nat = pl.BlockSpec((None, S, LANES), lambda b, h: (b, 0, h))
    perm = pl.BlockSpec((None, n_res, sub_len, LANES), lambda b, h: (b, 0, 0, h))
    return pl.pallas_call(
        _attn_body,
        grid=(B, W // LANES),
        in_specs=[nat, nat, nat, perm, perm, perm, _full(near.shape), _full(far.shape)],
        out_specs=nat,
        out_shape=jax.ShapeDtypeStruct((B, S, W), BF16),
        scratch_shapes=[pltpu.VMEM((2, S, LANES), BF16),
                        pltpu.VMEM((2, n_res, sub_len, LANES), BF16),
                        pltpu.VMEM((2, S, LANES), F32),
                        pltpu.VMEM((2, S, LANES), F32)],
        compiler_params=_params("arbitrary", "arbitrary"),
        name="dilated_attn",
    )(q, k, v, qp, kp, vp, near, far)


def _norm_matmul_body(x_ref, g_ref, w32_ref, o_ref, w_ref):
    @pl.when(pl.program_id(1) == 0)
    def _():
        w_ref[...] = w32_ref[...].astype(BF16)

    hb = _rms(x_ref[...], g_ref[...]).astype(BF16)
    o_ref[...] = jnp.dot(hb, w_ref[...], preferred_element_type=F32).astype(o_ref.dtype)


def _mem_kv_proj(x, g, w):
    T, D = x.shape
    depth, _, N = w.shape
    tm = ROW_TILE
    per_layer = lambda p: pl.BlockSpec((None,) + p.shape[1:], lambda l, i: (l, 0, 0))
    return pl.pallas_call(
        _norm_matmul_body,
        grid=(depth, T // tm),
        in_specs=[pl.BlockSpec((tm, D), lambda l, i: (i, 0)), per_layer(g), per_layer(w)],
        out_specs=pl.BlockSpec((None, tm, N), lambda l, i: (l, i, 0)),
        out_shape=jax.ShapeDtypeStruct((depth, T, N), BF16),
        scratch_shapes=[_mxu_copy(w)],
        compiler_params=_params("arbitrary", "arbitrary"),
        name="mem_kv_proj",
    )(x, g, w)


def _mix_out_mem_body(ya_ref, yb_ref, yc_ref, x_ref, wout32_ref, gmix_ref, gpre_ref, wq32_ref, kv_ref,
                      wo32_ref, gpost_ref, o_ref, wout_ref, wq_ref, wo_ref):
    _cast_weights_once(2, [(wout32_ref, wout_ref), (wq32_ref, wq_ref), (wo32_ref, wo_ref)])
    tm, D = x_ref.shape
    hd = D // MEM_HEADS

    def stages(rows):
        y = jnp.concatenate([ya_ref[rows, :], yb_ref[rows, :], yc_ref[rows, :]], axis=-1)
        y = jnp.dot(y, wout_ref[...], preferred_element_type=F32)
        yield
        x = x_ref[rows, :] + _rms(y, gmix_ref[...])
        hb = _rms(x, gpre_ref[...]).astype(BF16)
        yield
        q = jnp.dot(hb, wq_ref[...], preferred_element_type=F32) * (hd ** -0.5 * LOG2E)
        q = q.astype(BF16)
        scores = [_scores(q[:, h * hd:(h + 1) * hd], kv_ref[:, h * hd:(h + 1) * hd])
                  for h in range(MEM_HEADS)]
        yield
        probs = []
        for s in scores:
            p = jnp.exp2(s - jnp.max(s, axis=-1, keepdims=True))
            probs.append((p.astype(BF16), jnp.sum(p, axis=-1, keepdims=True)))
        yield
        outs = []
        for h, (p, den) in enumerate(probs):
            vh = kv_ref[:, D + h * hd:D + (h + 1) * hd]
            outs.append((jnp.dot(p, vh, preferred_element_type=F32) / den).astype(BF16))
        y = jnp.dot(jnp.concatenate(outs, axis=-1), wo_ref[...], preferred_element_type=F32)
        yield
        o_ref[rows, :] = x + _rms(y, gpost_ref[...])

    half = tm // MIX_ROW_GROUPS
    chains = [stages(slice(g * half, (g + 1) * half)) for g in range(MIX_ROW_GROUPS)]
    while chains:
        chains = [c for c in chains if next(c, StopIteration) is not StopIteration]


def _mix_out_mem(l, ya, yb, yc, x, wout, gmix, gpre, wq, kv, wo, gpost):
    B, S, D = x.shape
    M = kv.shape[2]
    tm = ROW_TILE
    row = lambda n: pl.BlockSpec((None, tm, n), lambda b, s: (b, s, 0))
    return pl.pallas_call(
        _mix_out_mem_body,
        grid=(B, S // tm),
        in_specs=[row(ya.shape[-1]), row(yb.shape[-1]), row(yc.shape[-1]), row(D),
                  _layer(wout, l), _layer(gmix, l), _layer(gpre, l), _layer(wq, l),
                  pl.BlockSpec((None, None, M, 2 * D), lambda b, s: (l, b, 0, 0)),
                  _layer(wo, l), _layer(gpost, l)],
        out_specs=row(D),
        out_shape=jax.ShapeDtypeStruct((B, S, D), F32),
        scratch_shapes=[_mxu_copy(wout), _mxu_copy(wq), _mxu_copy(wo)],
        compiler_params=_params("arbitrary", "arbitrary"),
        name="mix_out_mem_attn",
    )(ya, yb, yc, x, wout, gmix, gpre, wq, kv, wo, gpost)


def _ffn_body(x_ref, gpre_ref, wup_ref, cw_ref, cb_ref, wd_ref, gpost_ref, o_ref,
              carry, act, *, d_ff, tf):
    tm = x_ref.shape[0]
    pad = SUBLANES
    n_chunks = d_ff // tf
    first_rows = lax.broadcasted_iota(jnp.int32, (pad, tf), 0)

    @pl.when(pl.program_id(1) == 0)
    def _():
        carry[...] = jnp.zeros(carry.shape, F32)

    x = x_ref[...]
    hb = _rms(x, gpre_ref[...]).astype(BF16)

    def conv(col0):
        cols = slice(col0, col0 + tf)
        u = jnp.dot(hb, wup_ref[:, cols], preferred_element_type=F32)
        last = carry[:, cols]
        carry[:, cols] = u[tm - pad:tm, :]
        w = cw_ref[:, cols]
        y = cb_ref[:, cols] + w[FFN_CONV_KERNEL - 1:FFN_CONV_KERNEL, :] * u
        for t in range(FFN_CONV_KERNEL - 1):
            back = FFN_CONV_KERNEL - 1 - t
            shifted = pltpu.roll(u, back, axis=0)
            head = jnp.where(first_rows < back, pltpu.roll(last, back, axis=0), shifted[0:pad, :])
            y = y + w[t:t + 1, :] * jnp.concatenate([head, shifted[pad:, :]], axis=0)
        return y

    for c in range(n_chunks):
        gate = conv(c * tf)
        up = conv(d_ff + c * tf)
        act[:, c * tf:(c + 1) * tf] = (_gelu_tanh(gate) * up).astype(BF16)
    y = jnp.dot(act[...], wd_ref[...], preferred_element_type=F32)
    o_ref[...] = x + _rms(y, gpost_ref[...])


def _ffn(l, x, gpre, wup, cw, cb, wd, gpost):
    B, S, D = x.shape
    d_ff = wd.shape[1]
    tm = ROW_TILE
    tf = MXU_DIM
    row = pl.BlockSpec((None, tm, D), lambda b, s: (b, s, 0))
    return pl.pallas_call(
        functools.partial(_ffn_body, d_ff=d_ff, tf=tf),
        grid=(B, S // tm),
        in_specs=[row, _layer(gpre, l), _layer(wup, 0), _layer(cw, l), _layer(cb, l),
                  _layer(wd, 0), _layer(gpost, l)],
        out_specs=row,
        out_shape=jax.ShapeDtypeStruct((B, S, D), F32),
        scratch_shapes=[pltpu.VMEM((SUBLANES, 2 * d_ff), F32),
                        pltpu.VMEM((tm, d_ff), BF16)],
        compiler_params=_params("arbitrary", "arbitrary"),
        name="conv_ffn",
    )(x, gpre, wup, cw, cb, wd, gpost)


def _rope_tables(seq, heads):
    inv = 1.0 / (ROPE_THETA ** (jnp.arange(0, HEAD_DIM, 2, dtype=F32) / HEAD_DIM))
    ang = jnp.arange(seq, dtype=F32)[:, None] * inv[None, :]
    cos, sin = jnp.cos(ang), jnp.sin(ang)
    cos_h = jnp.concatenate([cos, cos], axis=-1)
    sin_h = jnp.concatenate([-sin, sin], axis=-1)
    return jnp.tile(cos_h, (1, heads)), jnp.tile(sin_h, (1, heads))


def _block_diag(w):
    depth, h, n, _ = w.shape
    eye = jnp.eye(h, dtype=w.dtype)
    return jnp.einsum('lhij,hg->lhigj', w, eye).reshape(depth, h * n, h * n)


def kernel(x, mem, g_mix_pre, w_in, conv_w, conv_b, conv_ln_g, conv_ln_b, lru_conv_w, lru_conv_b, lru_w_a, lru_b_a, lru_w_i, lru_b_i, lru_lambda, w_out, g_mix_post, g_mem_pre, g_mem_kv, w_mem_q, w_mem_kv, w_mem_o, g_mem_post, g_ffn_pre, w_up, ffn_conv_w, ffn_conv_b, w_down, g_ffn_post):
    B, S, D = x.shape
    depth = w_in.shape[0]
    conv_ch = conv_w.shape[-1]
    lru_w = lru_conv_w.shape[-1]
    att_w = w_out.shape[1] - conv_ch - lru_w
    n_mem = mem.shape[1]
    assert S % ROW_TILE == 0 and ROW_TILE % (ATT_PERM * SUBLANES * 2) == 0
    assert att_w % LANES == 0 and LANES == 2 * HEAD_DIM and (B * n_mem) % ROW_TILE == 0

    cos, sin = _rope_tables(S, att_w // HEAD_DIM)
    near, far = _attn_bias_tables(S)
    vec = lambda p: p.reshape(depth, 1, -1)
    mxu = lambda p: p.astype(BF16)
    mem2 = mem.reshape(B * n_mem, D)

    g_mix_pre, g_mix_post, g_mem_pre, g_mem_kv, g_mem_post, g_ffn_pre, g_ffn_post = map(
        vec, (g_mix_pre, g_mix_post, g_mem_pre, g_mem_kv, g_mem_post, g_ffn_pre, g_ffn_post))
    conv_b, conv_ln_g, conv_ln_b, lru_conv_b, lru_b_a, lru_b_i, lru_lambda, ffn_conv_b = map(
        vec, (conv_b, conv_ln_g, conv_ln_b, lru_conv_b, lru_b_a, lru_b_i, lru_lambda, ffn_conv_b))
    lru_w_a = mxu(_block_diag(lru_w_a))
    lru_w_i = mxu(_block_diag(lru_w_i))
    kv = _mem_kv_proj(mem2, g_mem_kv, w_mem_kv).reshape(depth, B, n_mem, 2 * D)

    for l in range(depth):
        y_a, bx, bg, q, k, v, qp, kp, vp, w_up_l, w_down_l = _in_proj(
            l, x, g_mix_pre, w_in, cos, sin, conv_w, conv_b, conv_ln_g, conv_ln_b, w_up, w_down,
            conv_ch, lru_w, att_w)
        y_b = _lru_mixer(l, bx, bg, lru_conv_w, lru_conv_b, lru_w_a, lru_b_a, lru_w_i, lru_b_i,
                         lru_lambda)
        y_c = _dilated_attention(q, k, v, qp, kp, vp, near, far)
        x = _mix_out_mem(l, y_a, y_b, y_c, x, w_out, g_mix_post, g_mem_pre, w_mem_q, kv, w_mem_o,
                         g_mem_post)
        x = _ffn(l, x, g_ffn_pre, w_up_l, ffn_conv_w, ffn_conv_b, w_down_l, g_ffn_post)
    return x
```

```python
import functools
import math

import numpy as np
import jax
import jax.numpy as jnp
from jax import lax
from jax.experimental import pallas as pl
from jax.experimental.pallas import tpu as pltpu

F32 = jnp.float32
BF16 = jnp.bfloat16

HEAD_DIM = 64
CONV_KERNEL = 31
LRU_CONV_KERNEL = 4
LRU_C = 8.0
DILATED_PATTERNS = ((128, 1), (512, 4), (2048, 16))
ROPE_THETA = 10000.0
MEM_HEADS = 4
FFN_CONV_KERNEL = 3
EPS = 1e-6

LANES = 128
SUBLANES = 8
MXU_DIM = 256
VMEM_LIMIT_BYTES = 56 * 1024 * 1024

ROW_TILE = 512
FFN_ROW_TILE = 1024
MIX_ROW_TILE = 1024
ATT_BLOCK = 128
ATT_PERM = 4
ATT_PIPELINE = 4
CONV_HALO = 32
CONV_SUB = 64
MIX_ROW_GROUPS = 2
LOG2E = 1.4426950408889634
SCAN_CHUNKS = SUBLANES
NEG_BIAS = -1e30


def _params(*sem):
    return pltpu.CompilerParams(dimension_semantics=sem, vmem_limit_bytes=VMEM_LIMIT_BYTES)


def _rms(x, g):
    return x * lax.rsqrt(jnp.mean(x * x, axis=-1, keepdims=True) + EPS) * g


def _sigmoid(x):
    return 1.0 / (1.0 + jnp.exp2(x * (-LOG2E)))


def _gelu_tanh(x):
    k = -2.0 * math.sqrt(2.0 / math.pi) * LOG2E
    return x / (1.0 + jnp.exp2(x * (k + (0.044715 * k) * (x * x))))


def _full(shape):
    n = len(shape)
    return pl.BlockSpec(shape, lambda *_: (0,) * n)


def _layer(p, l):
    return pl.BlockSpec((None,) + p.shape[1:], lambda *_: (l, 0, 0), pipeline_mode=pl.Buffered(1))


def _cast_weights_once(grid_rank, pairs):
    first = pl.program_id(0) == 0
    for axis in range(1, grid_rank):
        first = jnp.logical_and(first, pl.program_id(axis) == 0)

    @pl.when(first)
    def _():
        for src, dst in pairs:
            dst[...] = src[...].astype(dst.dtype)


def _mxu_copy(w):
    return pltpu.VMEM(w.shape[1:], BF16)


def _rope(t, cos, sin_signed, first_half):
    n = t.shape[-1]
    half = HEAD_DIM // 2
    partner = jnp.where(first_half, pltpu.roll(t, n - half, axis=1), pltpu.roll(t, half, axis=1))
    return t * cos + partner * sin_signed


def _store_attn_operand(t, nat_ref, perm_ref, stage):
    rows, width = t.shape
    nat_ref[...] = t.astype(BF16)
    for c in range(width // LANES):
        stage[c] = t[:, c * LANES:(c + 1) * LANES]
    for c in range(width // LANES):
        for r in range(ATT_PERM):
            sub = stage[c, pl.ds(r, rows // ATT_PERM, stride=ATT_PERM), :]
            perm_ref[r, :, c * LANES:(c + 1) * LANES] = sub.astype(BF16)


def _conv_mixer_halo(ubuf, tm):
    @pl.when(pl.program_id(1) == 0)
    def _():
        ubuf[0:CONV_HALO, :] = jnp.zeros((CONV_HALO, ubuf.shape[-1]), F32)

    @pl.when(pl.program_id(1) > 0)
    def _():
        ubuf[0:CONV_HALO, :] = ubuf[tm:tm + CONV_HALO, :]


def _conv_mixer_prepare(a, ubuf, ush, *, ch):
    tm = a.shape[0]
    ubuf[CONV_HALO:CONV_HALO + tm, :] = a[:, :ch] * _sigmoid(a[:, ch:])
    span = CONV_HALO + tm - SUBLANES
    for s in range(1, SUBLANES):
        ush[s - 1, 0:span, :] = ubuf[s:s + span, :]


def _conv_mixer_rows(r0, w_ref, b_ref, g_ref, beta_ref, o_ref, ubuf, ush):
    ch = ubuf.shape[-1]
    w = w_ref[...]
    off = CONV_HALO - (CONV_KERNEL - 1)
    acc = jnp.zeros((CONV_SUB, ch), F32) + b_ref[...]
    for j in range(CONV_KERNEL):
        s = (off + j) % SUBLANES
        lo = r0 + off + j - s
        win = ubuf[lo:lo + CONV_SUB, :] if s == 0 else ush[s - 1, lo:lo + CONV_SUB, :]
        acc = acc + w[j:j + 1, :] * win
    mu = jnp.mean(acc, axis=-1, keepdims=True)
    cen = acc - mu
    var = jnp.mean(cen * cen, axis=-1, keepdims=True)
    y = cen * lax.rsqrt(var + EPS) * g_ref[...] + beta_ref[...]
    o_ref[r0:r0 + CONV_SUB, :] = (y * _sigmoid(y)).astype(o_ref.dtype)


def _in_proj_body(x_ref, g_ref, w32_ref, cos_ref, sin_ref, cw_ref, cb_ref, cg_ref, cbeta_ref,
                  wup32_ref, wdown32_ref,
                  ya_ref, bx_ref, bg_ref, q_ref, k_ref, v_ref, qp_ref, kp_ref, vp_ref,
                  wup_ref, wdown_ref,
                  w_ref, ubuf, ush, stage_q, stage_k, stage_v, *, splits, n_down_pieces):
    tm = x_ref.shape[0]
    _cast_weights_once(2, [(w32_ref, w_ref)])
    step = pl.program_id(0) * pl.num_programs(1) + pl.program_id(1)
    wup_ref[...] = wup32_ref[...].astype(BF16)

    @pl.when(step < n_down_pieces)
    def _():
        wdown_ref[...] = wdown32_ref[...].astype(BF16)

    _conv_mixer_halo(ubuf, tm)
    hb = _rms(x_ref[...], g_ref[...]).astype(BF16)

    def proj(lo, hi):
        return jnp.dot(hb, w_ref[:, lo:hi], preferred_element_type=F32)

    c_a, c_bx, c_bg, c_q, c_k, c_v = splits
    rows = pl.ds(pl.multiple_of(pl.program_id(1) * tm, tm), tm)
    cos = cos_ref[rows, :]
    sin = sin_ref[rows, :]
    lane = lax.broadcasted_iota(jnp.int32, cos.shape, 1)
    first_half = (lane % HEAD_DIM) < (HEAD_DIM // 2)
    _conv_mixer_prepare(proj(0, c_a), ubuf, ush, ch=c_a // 2)
    for r0 in range(0, tm, CONV_SUB):
        _conv_mixer_rows(r0, cw_ref, cb_ref, cg_ref, cbeta_ref, ya_ref, ubuf, ush)
    bx_ref[...] = proj(c_a, c_bx)
    bg_ref[...] = proj(c_bx, c_bg)
    q = _rope(proj(c_bg, c_q), cos, sin, first_half) * (HEAD_DIM ** -0.5 * LOG2E)
    _store_attn_operand(q, q_ref, qp_ref, stage_q)
    _store_attn_operand(_rope(proj(c_q, c_k), cos, sin, first_half), k_ref, kp_ref, stage_k)
    _store_attn_operand(proj(c_k, c_v), v_ref, vp_ref, stage_v)


def _in_proj(l, x, g, w, cos, sin, conv_w, conv_b, conv_g, conv_beta, w_up, w_down,
             conv_ch, lru_w, att_w):
    B, S, D = x.shape
    tm = ROW_TILE
    n_steps = B * (S // tm)
    bf16_rows = 2 * SUBLANES
    up_rows = w_up.shape[1] // n_steps
    down_rows = LANES
    n_down = w_down.shape[1] // down_rows
    assert up_rows * n_steps == w_up.shape[1] and up_rows % bf16_rows == 0
    assert n_down * down_rows == w_down.shape[1] and n_down <= n_steps
    steps_per_b = S // tm
    up_piece = lambda b, s: (l, b * steps_per_b + s, 0)
    down_piece = lambda b, s: (l, jnp.minimum(b * steps_per_b + s, n_down - 1), 0)
    splits = (2 * conv_ch, 2 * conv_ch + lru_w, 2 * conv_ch + 2 * lru_w,
              2 * conv_ch + 2 * lru_w + att_w, 2 * conv_ch + 2 * lru_w + 2 * att_w,
              2 * conv_ch + 2 * lru_w + 3 * att_w)
    row = lambda n: pl.BlockSpec((None, tm, n), lambda b, s: (b, s, 0))
    perm = pl.BlockSpec((None, ATT_PERM, tm // ATT_PERM, att_w), lambda b, s: (b, 0, s, 0))
    nat_shape = jax.ShapeDtypeStruct((B, S, att_w), BF16)
    perm_shape = jax.ShapeDtypeStruct((B, ATT_PERM, S // ATT_PERM, att_w), BF16)
    stage = pltpu.VMEM((att_w // LANES, tm, LANES), F32)
    return pl.pallas_call(
        functools.partial(_in_proj_body, splits=splits, n_down_pieces=n_down),
        grid=(B, S // tm),
        in_specs=[row(D), _layer(g, l), _layer(w, l), _full(cos.shape), _full(sin.shape),
                  _layer(conv_w, l), _layer(conv_b, l), _layer(conv_g, l), _layer(conv_beta, l),
                  pl.BlockSpec((None, up_rows, w_up.shape[2]), up_piece),
                  pl.BlockSpec((None, down_rows, w_down.shape[2]), down_piece)],
        out_specs=[row(conv_ch), row(lru_w), row(lru_w), row(att_w), row(att_w), row(att_w),
                   perm, perm, perm,
                   pl.BlockSpec((None, up_rows, w_up.shape[2]), lambda b, s: (0,) + up_piece(b, s)[1:]),
                   pl.BlockSpec((None, down_rows, w_down.shape[2]),
                                lambda b, s: (0,) + down_piece(b, s)[1:])],
        out_shape=[jax.ShapeDtypeStruct((B, S, conv_ch), BF16),
                   jax.ShapeDtypeStruct((B, S, lru_w), F32),
                   jax.ShapeDtypeStruct((B, S, lru_w), F32),
                   nat_shape, nat_shape, nat_shape, perm_shape, perm_shape, perm_shape,
                   jax.ShapeDtypeStruct((1,) + w_up.shape[1:], BF16),
                   jax.ShapeDtypeStruct((1,) + w_down.shape[1:], BF16)],
        scratch_shapes=[_mxu_copy(w),
                        pltpu.VMEM((CONV_HALO + tm, conv_ch), F32),
                        pltpu.VMEM((SUBLANES - 1, CONV_HALO + tm - SUBLANES, conv_ch), F32),
                        stage, stage, stage],
        compiler_params=_params("arbitrary", "arbitrary"),
        name="in_proj",
    )(x, g, w, cos, sin, conv_w, conv_b, conv_g, conv_beta, w_up, w_down)


def _lru_body(bx_ref, bg_ref, cw_ref, cb_ref, wa_ref, ba_ref, wi_ref, bi_ref, lam_ref, o_ref,
              a_s, b_s, *, pitch):
    S, C = bx_ref.shape
    n_slab = C // LANES
    chunk = S // SCAN_CHUNKS
    pad = SUBLANES
    first_rows = lax.broadcasted_iota(jnp.int32, (pad, C), 0)
    z = -lam_ref[...]
    log_a_scale = (-LRU_C) * (jnp.maximum(z, 0.0) + jnp.log1p(jnp.exp(-jnp.abs(z))))
    cw = cw_ref[...]
    for j in range(SCAN_CHUNKS):
        base = j * chunk
        xc = bx_ref[base:base + chunk, :]
        last = bx_ref[base - pad:base, :] if j > 0 else jnp.zeros((pad, C), F32)
        u = cb_ref[...] + cw[LRU_CONV_KERNEL - 1:LRU_CONV_KERNEL, :] * xc
        for t in range(LRU_CONV_KERNEL - 1):
            back = LRU_CONV_KERNEL - 1 - t
            shifted = pltpu.roll(xc, back, axis=0)
            head = jnp.where(first_rows < back, pltpu.roll(last, back, axis=0), shifted[0:pad, :])
            u = u + cw[t:t + 1, :] * jnp.concatenate([head, shifted[pad:, :]], axis=0)
        ub = u.astype(BF16)
        r = _sigmoid(jnp.dot(ub, wa_ref[...], preferred_element_type=F32) + ba_ref[...])
        gate_i = _sigmoid(jnp.dot(ub, wi_ref[...], preferred_element_type=F32) + bi_ref[...])
        log_a = r * log_a_scale
        a = jnp.exp(log_a)
        bterm = jnp.sqrt(jnp.tanh(-log_a) * (1.0 + a * a)) * (gate_i * u)
        for s in range(n_slab):
            a_s[s, j * pitch:j * pitch + chunk, :] = a[:, s * LANES:(s + 1) * LANES]
            b_s[s, j * pitch:j * pitch + chunk, :] = bterm[:, s * LANES:(s + 1) * LANES]

    def step(i, carry):
        out = []
        for s in range(n_slab):
            h, p = carry[2 * s], carry[2 * s + 1]
            idx = pl.ds(i, SCAN_CHUNKS, stride=pitch)
            av = a_s[s, idx, :]
            h = av * h + b_s[s, idx, :]
            p = av * p
            b_s[s, idx, :] = h
            a_s[s, idx, :] = p
            out += [h, p]
        return tuple(out)

    init = []
    for s in range(n_slab):
        init += [jnp.zeros((SCAN_CHUNKS, LANES), F32), jnp.ones((SCAN_CHUNKS, LANES), F32)]
    fin = lax.fori_loop(0, chunk, step, tuple(init), unroll=8)

    for s in range(n_slab):
        h_end, p_end = fin[2 * s], fin[2 * s + 1]
        carry_in = jnp.zeros((1, LANES), F32)
        for j in range(SCAN_CHUNKS):
            rows = slice(j * pitch, j * pitch + chunk)
            h = b_s[s, rows, :] + a_s[s, rows, :] * carry_in
            gate = _gelu_tanh(bg_ref[j * chunk:(j + 1) * chunk, s * LANES:(s + 1) * LANES])
            o_ref[j * chunk:(j + 1) * chunk, s * LANES:(s + 1) * LANES] = (h * gate).astype(o_ref.dtype)
            carry_in = h_end[j:j + 1, :] + p_end[j:j + 1, :] * carry_in


def _lru_mixer(l, bx, bg, cw, cb, wa, ba, wi, bi, lam):
    B, S, C = bx.shape
    chunk = S // SCAN_CHUNKS
    pitch = chunk + SUBLANES
    seq = pl.BlockSpec((None, S, C), lambda b: (b, 0, 0))
    params = (cw, cb, wa, ba, wi, bi, lam)
    return pl.pallas_call(
        functools.partial(_lru_body, pitch=pitch),
        grid=(B,),
        in_specs=[seq, seq] + [_layer(p, l) for p in params],
        out_specs=seq,
        out_shape=jax.ShapeDtypeStruct((B, S, C), BF16),
        scratch_shapes=[pltpu.VMEM((C // LANES, SCAN_CHUNKS * pitch, LANES), F32),
                        pltpu.VMEM((C // LANES, SCAN_CHUNKS * pitch, LANES), F32)],
        compiler_params=_params("arbitrary"),
        name="lru_mixer",
    )(bx, bg, *params)


def _attn_bias_tables(seq):
    blk = ATT_BLOCK
    local = [(w, d) for w, d in DILATED_PATTERNS if d == 1]
    strided = [(w, d) for w, d in DILATED_PATTERNS if d != 1]
    for window, dil in DILATED_PATTERNS:
        n = window // dil
        assert seq % dil == 0 and (seq // dil) % n == 0, "pattern padding not supported"
    assert all(w <= blk for w, _ in local) and all(d % ATT_PERM == 0 for _, d in strided)
    assert seq % (ATT_PERM * blk) == 0

    def bias(mult):
        return jnp.asarray(np.where(mult > 0, np.log2(np.maximum(mult, 1)), NEG_BIAS), F32)

    r = np.arange(blk)[:, None]
    d = r - np.arange(2 * blk)[None, :] + blk
    near = sum(((d >= 0) & (d <= w)).astype(np.int64) for w, _ in local)
    n_sub = seq // ATT_PERM // blk
    d = ATT_PERM * (r - np.arange(n_sub * blk)[None, :] + (n_sub - 1) * blk)
    far = sum(((d >= 0) & (d % dil == 0) & (d <= w)).astype(np.int64) for w, dil in strided)
    return bias(near), bias(far)


def _scores(q, k):
    return lax.dot_general(q, k, (((1,), (1,)), ((), ())), preferred_element_type=F32)


def _attn_body(q_ref, k_ref, v_ref, qp_ref, kp_ref, vp_ref, near_ref, far_ref, o_ref,
               vn, vp, acc_far, max_far):
    S = q_ref.shape[0]
    blk = ATT_BLOCK
    n_res, sub_len = qp_ref.shape[0], qp_ref.shape[1]
    n_sub = sub_len // blk
    head0 = lax.broadcasted_iota(jnp.int32, (blk, LANES), 1) < HEAD_DIM
    head0_sub = lax.broadcasted_iota(jnp.int32, (sub_len, LANES), 1) < HEAD_DIM
    one = jnp.ones((sub_len, LANES), BF16)
    for h in range(2):
        own = head0_sub if h == 0 else jnp.logical_not(head0_sub)
        for r in range(n_res):
            vp[h, r] = jnp.where(own, vp_ref[r], one)
        for c in range(S // sub_len):
            rows = slice(c * sub_len, (c + 1) * sub_len)
            vn[h, rows, :] = jnp.where(own, v_ref[rows, :], one)

    def own_lanes(h):
        return head0 if h == 0 else jnp.logical_not(head0)

    def far_block(h, r, a):
        nk = (a + 1) * blk

        def scores():
            q = qp_ref[r, a * blk:(a + 1) * blk, :]
            q = jnp.where(own_lanes(h), q, jnp.zeros_like(q))
            return _scores(q, kp_ref[r, 0:nk, :]) + far_ref[:, (n_sub - 1 - a) * blk:]

        def finish(s):
            m = jnp.max(s, axis=-1, keepdims=True)
            p = jnp.exp2(s - m).astype(BF16)
            rows = pl.ds(r + n_res * a * blk, blk, stride=n_res)
            acc_far[h, rows, :] = jnp.dot(p, vp[h, r, 0:nk, :], preferred_element_type=F32)
            max_far[h, rows, :] = jnp.broadcast_to(m, (blk, LANES))

        return scores, finish

    def near_block(i):
        rows = slice(i * blk, (i + 1) * blk)
        keys = slice(max(i - 1, 0) * blk, (i + 1) * blk)
        bias = near_ref[...] if i > 0 else near_ref[:, blk:]

        def scores():
            q = q_ref[rows, :]
            return [_scores(jnp.where(own_lanes(h), q, jnp.zeros_like(q)), k_ref[keys, :]) + bias
                    for h in range(2)]

        def finish(s_heads):
            acc = []
            for h, s in enumerate(s_heads):
                m_far = max_far[h, rows, :]
                m = jnp.maximum(jnp.max(s, axis=-1, keepdims=True), m_far)
                m_keys = m if i == 0 else jnp.concatenate([m, m], axis=-1)
                p = jnp.exp2(s - m_keys).astype(BF16)
                acc.append(jnp.dot(p, vn[h, keys, :], preferred_element_type=F32)
                           + jnp.exp2(m_far - m) * acc_far[h, rows, :])
            num = jnp.where(head0, acc[0], acc[1])
            den = pltpu.roll(jnp.where(head0, acc[1], acc[0]), HEAD_DIM, axis=1)
            o_ref[rows, :] = (num / den).astype(o_ref.dtype)

        return scores, finish

    blocks = []
    for a in range(n_sub):
        blocks += [far_block(h, r, a) for r in range(n_res) for h in range(2)]
        blocks += [near_block(i) for i in range(a * n_res, (a + 1) * n_res)]
    pending = []
    for scores, finish in blocks:
        pending.append((finish, scores()))
        if len(pending) > ATT_PIPELINE:
            fn, s = pending.pop(0)
            fn(s)
    for fn, s in pending:
        fn(s)


def _dilated_attention(q, k, v, qp, kp, vp, near, far):
    B, S, W = q.shape
    n_res, sub_len = qp.shape[1], qp.shape[2]
    nat = pl.BlockSpec((None, S, LANES), lambda b, h: (b, 0, h))
    perm = pl.BlockSpec((None, n_res, sub_len, LANES), lambda b, h: (b, 0, 0, h))
    return pl.pallas_call(
        _attn_body,
        grid=(B, W // LANES),
        in_specs=[nat, nat, nat, perm, perm, perm, _full(near.shape), _full(far.shape)],
        out_specs=nat,
        out_shape=jax.ShapeDtypeStruct((B, S, W), BF16),
        scratch_shapes=[pltpu.VMEM((2, S, LANES), BF16),
                        pltpu.VMEM((2, n_res, sub_len, LANES), BF16),
                        pltpu.VMEM((2, S, LANES), F32),
                        pltpu.VMEM((2, S, LANES), F32)],
        compiler_params=_params("arbitrary", "arbitrary"),
        name="dilated_attn",
    )(q, k, v, qp, kp, vp, near, far)


def _norm_matmul_body(x_ref, g_ref, w32_ref, o_ref, w_ref):
    @pl.when(pl.program_id(1) == 0)
    def _():
        w_ref[...] = w32_ref[...].astype(BF16)

    hb = _rms(x_ref[...], g_ref[...]).astype(BF16)
    o_ref[...] = jnp.dot(hb, w_ref[...], preferred_element_type=F32).astype(o_ref.dtype)


def _mem_kv_proj(x, g, w):
    T, D = x.shape
    depth, _, N = w.shape
    tm = ROW_TILE
    per_layer = lambda p: pl.BlockSpec((None,) + p.shape[1:], lambda l, i: (l, 0, 0))
    return pl.pallas_call(
        _norm_matmul_body,
        grid=(depth, T // tm),
        in_specs=[pl.BlockSpec((tm, D), lambda l, i: (i, 0)), per_layer(g), per_layer(w)],
        out_specs=pl.BlockSpec((None, tm, N), lambda l, i: (l, i, 0)),
        out_shape=jax.ShapeDtypeStruct((depth, T, N), BF16),
        scratch_shapes=[_mxu_copy(w)],
        compiler_params=_params("arbitrary", "arbitrary"),
        name="mem_kv_proj",
    )(x, g, w)


def _mix_out_mem_body(ya_ref, yb_ref, yc_ref, x_ref, wout32_ref, gmix_ref, gpre_ref, wq32_ref, kv_ref,
                      wo32_ref, gpost_ref, o_ref, wout_ref, wq_ref, wo_ref):
    _cast_weights_once(2, [(wout32_ref, wout_ref), (wq32_ref, wq_ref), (wo32_ref, wo_ref)])
    tm, D = x_ref.shape
    hd = D // MEM_HEADS

    def stages(rows):
        y = jnp.concatenate([ya_ref[rows, :], yb_ref[rows, :], yc_ref[rows, :]], axis=-1)
        y = jnp.dot(y, wout_ref[...], preferred_element_type=F32)
        yield
        x = x_ref[rows, :] + _rms(y, gmix_ref[...])
        hb = _rms(x, gpre_ref[...]).astype(BF16)
        yield
        q = jnp.dot(hb, wq_ref[...], preferred_element_type=F32) * (hd ** -0.5 * LOG2E)
        q = q.astype(BF16)
        scores = [_scores(q[:, h * hd:(h + 1) * hd], kv_ref[:, h * hd:(h + 1) * hd])
                  for h in range(MEM_HEADS)]
        yield
        probs = []
        for s in scores:
            p = jnp.exp2(s - jnp.max(s, axis=-1, keepdims=True))
            probs.append((p.astype(BF16), jnp.sum(p, axis=-1, keepdims=True)))
        yield
        outs = []
        for h, (p, den) in enumerate(probs):
            vh = kv_ref[:, D + h * hd:D + (h + 1) * hd]
            outs.append((jnp.dot(p, vh, preferred_element_type=F32) / den).astype(BF16))
        y = jnp.dot(jnp.concatenate(outs, axis=-1), wo_ref[...], preferred_element_type=F32)
        yield
        o_ref[rows, :] = x + _rms(y, gpost_ref[...])

    half = tm // MIX_ROW_GROUPS
    chains = [stages(slice(g * half, (g + 1) * half)) for g in range(MIX_ROW_GROUPS)]
    while chains:
        chains = [c for c in chains if next(c, StopIteration) is not StopIteration]


def _mix_out_mem(l, ya, yb, yc, x, wout, gmix, gpre, wq, kv, wo, gpost):
    B, S, D = x.shape
    M = kv.shape[2]
    tm = MIX_ROW_TILE
    row = lambda n: pl.BlockSpec((None, tm, n), lambda b, s: (b, s, 0))
    return pl.pallas_call(
        _mix_out_mem_body,
        grid=(B, S // tm),
        in_specs=[row(ya.shape[-1]), row(yb.shape[-1]), row(yc.shape[-1]), row(D),
                  _layer(wout, l), _layer(gmix, l), _layer(gpre, l), _layer(wq, l),
                  pl.BlockSpec((None, None, M, 2 * D), lambda b, s: (l, b, 0, 0)),
                  _layer(wo, l), _layer(gpost, l)],
        out_specs=row(D),
        out_shape=jax.ShapeDtypeStruct((B, S, D), F32),
        scratch_shapes=[_mxu_copy(wout), _mxu_copy(wq), _mxu_copy(wo)],
        compiler_params=_params("arbitrary", "arbitrary"),
        name="mix_out_mem_attn",
    )(ya, yb, yc, x, wout, gmix, gpre, wq, kv, wo, gpost)


def _ffn_body(x_ref, gpre_ref, wup_ref, cw_ref, cb_ref, wd_ref, gpost_ref, o_ref,
              carry, act, *, d_ff, tf):
    tm = x_ref.shape[0]
    pad = SUBLANES
    n_chunks = d_ff // tf
    first_rows = lax.broadcasted_iota(jnp.int32, (pad, tf), 0)

    @pl.when(pl.program_id(1) == 0)
    def _():
        carry[...] = jnp.zeros(carry.shape, F32)

    x = x_ref[...]
    hb = _rms(x, gpre_ref[...]).astype(BF16)

    def conv(col0):
        cols = slice(col0, col0 + tf)
        u = jnp.dot(hb, wup_ref[:, cols], preferred_element_type=F32)
        last = carry[:, cols]
        carry[:, cols] = u[tm - pad:tm, :]
        w = cw_ref[:, cols]
        y = cb_ref[:, cols] + w[FFN_CONV_KERNEL - 1:FFN_CONV_KERNEL, :] * u
        for t in range(FFN_CONV_KERNEL - 1):
            back = FFN_CONV_KERNEL - 1 - t
            shifted = pltpu.roll(u, back, axis=0)
            head = jnp.where(first_rows < back, pltpu.roll(last, back, axis=0), shifted[0:pad, :])
            y = y + w[t:t + 1, :] * jnp.concatenate([head, shifted[pad:, :]], axis=0)
        return y

    for c in range(n_chunks):
        gate = conv(c * tf)
        up = conv(d_ff + c * tf)
        act[:, c * tf:(c + 1) * tf] = (_gelu_tanh(gate) * up).astype(BF16)
    y = jnp.dot(act[...], wd_ref[...], preferred_element_type=F32)
    o_ref[...] = x + _rms(y, gpost_ref[...])


def _ffn(l, x, gpre, wup, cw, cb, wd, gpost):
    B, S, D = x.shape
    d_ff = wd.shape[1]
    tm = FFN_ROW_TILE
    tf = MXU_DIM
    row = pl.BlockSpec((None, tm, D), lambda b, s: (b, s, 0))
    return pl.pallas_call(
        functools.partial(_ffn_body, d_ff=d_ff, tf=tf),
        grid=(B, S // tm),
        in_specs=[row, _layer(gpre, l), _layer(wup, 0), _layer(cw, l), _layer(cb, l),
                  _layer(wd, 0), _layer(gpost, l)],
        out_specs=row,
        out_shape=jax.ShapeDtypeStruct((B, S, D), F32),
        scratch_shapes=[pltpu.VMEM((SUBLANES, 2 * d_ff), F32),
                        pltpu.VMEM((tm, d_ff), BF16)],
        compiler_params=_params("arbitrary", "arbitrary"),
        name="conv_ffn",
    )(x, gpre, wup, cw, cb, wd, gpost)


def _rope_tables(seq, heads):
    inv = 1.0 / (ROPE_THETA ** (jnp.arange(0, HEAD_DIM, 2, dtype=F32) / HEAD_DIM))
    ang = jnp.arange(seq, dtype=F32)[:, None] * inv[None, :]
    cos, sin = jnp.cos(ang), jnp.sin(ang)
    cos_h = jnp.concatenate([cos, cos], axis=-1)
    sin_h = jnp.concatenate([-sin, sin], axis=-1)
    return jnp.tile(cos_h, (1, heads)), jnp.tile(sin_h, (1, heads))


def _block_diag(w):
    depth, h, n, _ = w.shape
    eye = jnp.eye(h, dtype=w.dtype)
    return jnp.einsum('lhij,hg->lhigj', w, eye).reshape(depth, h * n, h * n)


def kernel(x, mem, g_mix_pre, w_in, conv_w, conv_b, conv_ln_g, conv_ln_b, lru_conv_w, lru_conv_b, lru_w_a, lru_b_a, lru_w_i, lru_b_i, lru_lambda, w_out, g_mix_post, g_mem_pre, g_mem_kv, w_mem_q, w_mem_kv, w_mem_o, g_mem_post, g_ffn_pre, w_up, ffn_conv_w, ffn_conv_b, w_down, g_ffn_post):
    B, S, D = x.shape
    depth = w_in.shape[0]
    conv_ch = conv_w.shape[-1]
    lru_w = lru_conv_w.shape[-1]
    att_w = w_out.shape[1] - conv_ch - lru_w
    n_mem = mem.shape[1]
    assert S % ROW_TILE == 0 and ROW_TILE % (ATT_PERM * SUBLANES * 2) == 0
    assert att_w % LANES == 0 and LANES == 2 * HEAD_DIM and (B * n_mem) % ROW_TILE == 0

    cos, sin = _rope_tables(S, att_w // HEAD_DIM)
    near, far = _attn_bias_tables(S)
    vec = lambda p: p.reshape(depth, 1, -1)
    mxu = lambda p: p.astype(BF16)
    mem2 = mem.reshape(B * n_mem, D)

    g_mix_pre, g_mix_post, g_mem_pre, g_mem_kv, g_mem_post, g_ffn_pre, g_ffn_post = map(
        vec, (g_mix_pre, g_mix_post, g_mem_pre, g_mem_kv, g_mem_post, g_ffn_pre, g_ffn_post))
    conv_b, conv_ln_g, conv_ln_b, lru_conv_b, lru_b_a, lru_b_i, lru_lambda, ffn_conv_b = map(
        vec, (conv_b, conv_ln_g, conv_ln_b, lru_conv_b, lru_b_a, lru_b_i, lru_lambda, ffn_conv_b))
    lru_w_a = mxu(_block_diag(lru_w_a))
    lru_w_i = mxu(_block_diag(lru_w_i))
    kv = _mem_kv_proj(mem2, g_mem_kv, w_mem_kv).reshape(depth, B, n_mem, 2 * D)

    for l in range(depth):
        y_a, bx, bg, q, k, v, qp, kp, vp, w_up_l, w_down_l = _in_proj(
            l, x, g_mix_pre, w_in, cos, sin, conv_w, conv_b, conv_ln_g, conv_ln_b, w_up, w_down,
            conv_ch, lru_w, att_w)
        y_b = _lru_mixer(l, bx, bg, lru_conv_w, lru_conv_b, lru_w_a, lru_b_a, lru_w_i, lru_b_i,
                         lru_lambda)
        y_c = _dilated_attention(q, k, v, qp, kp, vp, near, far)
        x = _mix_out_mem(l, y_a, y_b, y_c, x, w_out, g_mix_post, g_mem_pre, w_mem_q, kv, w_mem_o,
                         g_mem_post)
        x = _ffn(l, x, g_ffn_pre, w_up_l, ffn_conv_w, ffn_conv_b, w_down_l, g_ffn_post)
    return x
```

```python
import functools
import math

import numpy as np
import jax
import jax.numpy as jnp
from jax import lax
from jax.experimental import pallas as pl
from jax.experimental.pallas import tpu as pltpu

F32 = jnp.float32
BF16 = jnp.bfloat16

HEAD_DIM = 64
CONV_KERNEL = 31
LRU_CONV_KERNEL = 4
LRU_C = 8.0
DILATED_PATTERNS = ((128, 1), (512, 4), (2048, 16))
ROPE_THETA = 10000.0
MEM_HEADS = 4
FFN_CONV_KERNEL = 3
EPS = 1e-6

LANES = 128
SUBLANES = 8
MXU_DIM = 256
VMEM_LIMIT_BYTES = 56 * 1024 * 1024

ROW_TILE = 512
FFN_ROW_TILE = 1024
MIX_ROW_TILE = 1024
ATT_BLOCK = 128
ATT_PERM = 4
ATT_PAIRS_PER_STEP = 2
ATT_PIPELINE = 4
CONV_HALO = 32
CONV_SUB = 64
MIX_ROW_GROUPS = 2
LOG2E = 1.4426950408889634
SCAN_CHUNKS = SUBLANES
NEG_BIAS = -1e30


def _params(*sem):
    return pltpu.CompilerParams(dimension_semantics=sem, vmem_limit_bytes=VMEM_LIMIT_BYTES)


def _rms(x, g):
    return x * lax.rsqrt(jnp.mean(x * x, axis=-1, keepdims=True) + EPS) * g


def _sigmoid(x):
    return 1.0 / (1.0 + jnp.exp2(x * (-LOG2E)))


def _gelu_tanh(x):
    k = -2.0 * math.sqrt(2.0 / math.pi) * LOG2E
    return x / (1.0 + jnp.exp2(x * (k + (0.044715 * k) * (x * x))))


def _full(shape):
    n = len(shape)
    return pl.BlockSpec(shape, lambda *_: (0,) * n)


def _layer(p, l):
    return pl.BlockSpec((None,) + p.shape[1:], lambda *_: (l, 0, 0), pipeline_mode=pl.Buffered(1))


def _cast_weights_once(grid_rank, pairs):
    first = pl.program_id(0) == 0
    for axis in range(1, grid_rank):
        first = jnp.logical_and(first, pl.program_id(axis) == 0)

    @pl.when(first)
    def _():
        for src, dst in pairs:
            dst[...] = src[...].astype(dst.dtype)


def _mxu_copy(w):
    return pltpu.VMEM(w.shape[1:], BF16)


def _rope(t, cos, sin_signed, first_half):
    n = t.shape[-1]
    half = HEAD_DIM // 2
    partner = jnp.where(first_half, pltpu.roll(t, n - half, axis=1), pltpu.roll(t, half, axis=1))
    return t * cos + partner * sin_signed


def _store_attn_operand(t, nat_ref, perm_ref, stage):
    rows, width = t.shape
    nat_ref[...] = t.astype(BF16)
    for c in range(width // LANES):
        stage[c] = t[:, c * LANES:(c + 1) * LANES]
    for c in range(width // LANES):
        for r in range(ATT_PERM):
            sub = stage[c, pl.ds(r, rows // ATT_PERM, stride=ATT_PERM), :]
            perm_ref[r, :, c * LANES:(c + 1) * LANES] = sub.astype(BF16)


def _conv_mixer_halo(ubuf, tm):
    @pl.when(pl.program_id(1) == 0)
    def _():
        ubuf[0:CONV_HALO, :] = jnp.zeros((CONV_HALO, ubuf.shape[-1]), F32)

    @pl.when(pl.program_id(1) > 0)
    def _():
        ubuf[0:CONV_HALO, :] = ubuf[tm:tm + CONV_HALO, :]


def _conv_mixer_prepare(a, ubuf, ush, *, ch):
    tm = a.shape[0]
    ubuf[CONV_HALO:CONV_HALO + tm, :] = a[:, :ch] * _sigmoid(a[:, ch:])
    span = CONV_HALO + tm - SUBLANES
    for s in range(1, SUBLANES):
        ush[s - 1, 0:span, :] = ubuf[s:s + span, :]


def _conv_mixer_rows(r0, w_ref, b_ref, g_ref, beta_ref, o_ref, ubuf, ush):
    ch = ubuf.shape[-1]
    w = w_ref[...]
    off = CONV_HALO - (CONV_KERNEL - 1)
    acc = jnp.zeros((CONV_SUB, ch), F32) + b_ref[...]
    for j in range(CONV_KERNEL):
        s = (off + j) % SUBLANES
        lo = r0 + off + j - s
        win = ubuf[lo:lo + CONV_SUB, :] if s == 0 else ush[s - 1, lo:lo + CONV_SUB, :]
        acc = acc + w[j:j + 1, :] * win
    mu = jnp.mean(acc, axis=-1, keepdims=True)
    cen = acc - mu
    var = jnp.mean(cen * cen, axis=-1, keepdims=True)
    y = cen * lax.rsqrt(var + EPS) * g_ref[...] + beta_ref[...]
    o_ref[r0:r0 + CONV_SUB, :] = (y * _sigmoid(y)).astype(o_ref.dtype)


def _in_proj_body(x_ref, g_ref, w32_ref, cos_ref, sin_ref, cw_ref, cb_ref, cg_ref, cbeta_ref,
                  wup32_ref, wdown32_ref,
                  ya_ref, bx_ref, bg_ref, q_ref, k_ref, v_ref, qp_ref, kp_ref, vp_ref,
                  wup_ref, wdown_ref,
                  w_ref, ubuf, ush, stage_q, stage_k, stage_v, *, splits, n_down_pieces):
    tm = x_ref.shape[0]
    _cast_weights_once(2, [(w32_ref, w_ref)])
    step = pl.program_id(0) * pl.num_programs(1) + pl.program_id(1)
    wup_ref[...] = wup32_ref[...].astype(BF16)

    @pl.when(step < n_down_pieces)
    def _():
        wdown_ref[...] = wdown32_ref[...].astype(BF16)

    _conv_mixer_halo(ubuf, tm)
    hb = _rms(x_ref[...], g_ref[...]).astype(BF16)

    def proj(lo, hi):
        return jnp.dot(hb, w_ref[:, lo:hi], preferred_element_type=F32)

    c_a, c_bx, c_bg, c_q, c_k, c_v = splits
    rows = pl.ds(pl.multiple_of(pl.program_id(1) * tm, tm), tm)
    cos = cos_ref[rows, :]
    sin = sin_ref[rows, :]
    lane = lax.broadcasted_iota(jnp.int32, cos.shape, 1)
    first_half = (lane % HEAD_DIM) < (HEAD_DIM // 2)
    _conv_mixer_prepare(proj(0, c_a), ubuf, ush, ch=c_a // 2)
    for r0 in range(0, tm, CONV_SUB):
        _conv_mixer_rows(r0, cw_ref, cb_ref, cg_ref, cbeta_ref, ya_ref, ubuf, ush)
    bx_ref[...] = proj(c_a, c_bx)
    bg_ref[...] = proj(c_bx, c_bg)
    q = _rope(proj(c_bg, c_q), cos, sin, first_half) * (HEAD_DIM ** -0.5 * LOG2E)
    _store_attn_operand(q, q_ref, qp_ref, stage_q)
    _store_attn_operand(_rope(proj(c_q, c_k), cos, sin, first_half), k_ref, kp_ref, stage_k)
    _store_attn_operand(proj(c_k, c_v), v_ref, vp_ref, stage_v)


def _in_proj(l, x, g, w, cos, sin, conv_w, conv_b, conv_g, conv_beta, w_up, w_down,
             conv_ch, lru_w, att_w):
    B, S, D = x.shape
    tm = ROW_TILE
    n_steps = B * (S // tm)
    bf16_rows = 2 * SUBLANES
    up_rows = w_up.shape[1] // n_steps
    down_rows = LANES
    n_down = w_down.shape[1] // down_rows
    assert up_rows * n_steps == w_up.shape[1] and up_rows % bf16_rows == 0
    assert n_down * down_rows == w_down.shape[1] and n_down <= n_steps
    steps_per_b = S // tm
    up_piece = lambda b, s: (l, b * steps_per_b + s, 0)
    down_piece = lambda b, s: (l, jnp.minimum(b * steps_per_b + s, n_down - 1), 0)
    splits = (2 * conv_ch, 2 * conv_ch + lru_w, 2 * conv_ch + 2 * lru_w,
              2 * conv_ch + 2 * lru_w + att_w, 2 * conv_ch + 2 * lru_w + 2 * att_w,
              2 * conv_ch + 2 * lru_w + 3 * att_w)
    row = lambda n: pl.BlockSpec((None, tm, n), lambda b, s: (b, s, 0))
    perm = pl.BlockSpec((None, ATT_PERM, tm // ATT_PERM, att_w), lambda b, s: (b, 0, s, 0))
    nat_shape = jax.ShapeDtypeStruct((B, S, att_w), BF16)
    perm_shape = jax.ShapeDtypeStruct((B, ATT_PERM, S // ATT_PERM, att_w), BF16)
    stage = pltpu.VMEM((att_w // LANES, tm, LANES), F32)
    return pl.pallas_call(
        functools.partial(_in_proj_body, splits=splits, n_down_pieces=n_down),
        grid=(B, S // tm),
        in_specs=[row(D), _layer(g, l), _layer(w, l), _full(cos.shape), _full(sin.shape),
                  _layer(conv_w, l), _layer(conv_b, l), _layer(conv_g, l), _layer(conv_beta, l),
                  pl.BlockSpec((None, up_rows, w_up.shape[2]), up_piece),
                  pl.BlockSpec((None, down_rows, w_down.shape[2]), down_piece)],
        out_specs=[row(conv_ch), row(lru_w), row(lru_w), row(att_w), row(att_w), row(att_w),
                   perm, perm, perm,
                   pl.BlockSpec((None, up_rows, w_up.shape[2]), lambda b, s: (0,) + up_piece(b, s)[1:]),
                   pl.BlockSpec((None, down_rows, w_down.shape[2]),
                                lambda b, s: (0,) + down_piece(b, s)[1:])],
        out_shape=[jax.ShapeDtypeStruct((B, S, conv_ch), BF16),
                   jax.ShapeDtypeStruct((B, S, lru_w), F32),
                   jax.ShapeDtypeStruct((B, S, lru_w), F32),
                   nat_shape, nat_shape, nat_shape, perm_shape, perm_shape, perm_shape,
                   jax.ShapeDtypeStruct((1,) + w_up.shape[1:], BF16),
                   jax.ShapeDtypeStruct((1,) + w_down.shape[1:], BF16)],
        scratch_shapes=[_mxu_copy(w),
                        pltpu.VMEM((CONV_HALO + tm, conv_ch), F32),
                        pltpu.VMEM((SUBLANES - 1, CONV_HALO + tm - SUBLANES, conv_ch), F32),
                        stage, stage, stage],
        compiler_params=_params("arbitrary", "arbitrary"),
        name="in_proj",
    )(x, g, w, cos, sin, conv_w, conv_b, conv_g, conv_beta, w_up, w_down)


def _lru_body(bx_ref, bg_ref, cw_ref, cb_ref, wa_ref, ba_ref, wi_ref, bi_ref, lam_ref, o_ref,
              a_s, b_s, *, pitch):
    S, C = bx_ref.shape
    n_slab = C // LANES
    chunk = S // SCAN_CHUNKS
    pad = SUBLANES
    first_rows = lax.broadcasted_iota(jnp.int32, (pad, C), 0)
    z = -lam_ref[...]
    log_a_scale = (-LRU_C) * (jnp.maximum(z, 0.0) + jnp.log1p(jnp.exp(-jnp.abs(z))))
    cw = cw_ref[...]
    for j in range(SCAN_CHUNKS):
        base = j * chunk
        xc = bx_ref[base:base + chunk, :]
        last = bx_ref[base - pad:base, :] if j > 0 else jnp.zeros((pad, C), F32)
        u = cb_ref[...] + cw[LRU_CONV_KERNEL - 1:LRU_CONV_KERNEL, :] * xc
        for t in range(LRU_CONV_KERNEL - 1):
            back = LRU_CONV_KERNEL - 1 - t
            shifted = pltpu.roll(xc, back, axis=0)
            head = jnp.where(first_rows < back, pltpu.roll(last, back, axis=0), shifted[0:pad, :])
            u = u + cw[t:t + 1, :] * jnp.concatenate([head, shifted[pad:, :]], axis=0)
        ub = u.astype(BF16)
        r = _sigmoid(jnp.dot(ub, wa_ref[...], preferred_element_type=F32) + ba_ref[...])
        gate_i = _sigmoid(jnp.dot(ub, wi_ref[...], preferred_element_type=F32) + bi_ref[...])
        log_a = r * log_a_scale
        a = jnp.exp(log_a)
        bterm = jnp.sqrt(jnp.tanh(-log_a) * (1.0 + a * a)) * (gate_i * u)
        for s in range(n_slab):
            a_s[s, j * pitch:j * pitch + chunk, :] = a[:, s * LANES:(s + 1) * LANES]
            b_s[s, j * pitch:j * pitch + chunk, :] = bterm[:, s * LANES:(s + 1) * LANES]

    def step(i, carry):
        out = []
        for s in range(n_slab):
            h, p = carry[2 * s], carry[2 * s + 1]
            idx = pl.ds(i, SCAN_CHUNKS, stride=pitch)
            av = a_s[s, idx, :]
            h = av * h + b_s[s, idx, :]
            p = av * p
            b_s[s, idx, :] = h
            a_s[s, idx, :] = p
            out += [h, p]
        return tuple(out)

    init = []
    for s in range(n_slab):
        init += [jnp.zeros((SCAN_CHUNKS, LANES), F32), jnp.ones((SCAN_CHUNKS, LANES), F32)]
    fin = lax.fori_loop(0, chunk, step, tuple(init), unroll=8)

    for s in range(n_slab):
        h_end, p_end = fin[2 * s], fin[2 * s + 1]
        carry_in = jnp.zeros((1, LANES), F32)
        for j in range(SCAN_CHUNKS):
            rows = slice(j * pitch, j * pitch + chunk)
            h = b_s[s, rows, :] + a_s[s, rows, :] * carry_in
            gate = _gelu_tanh(bg_ref[j * chunk:(j + 1) * chunk, s * LANES:(s + 1) * LANES])
            o_ref[j * chunk:(j + 1) * chunk, s * LANES:(s + 1) * LANES] = (h * gate).astype(o_ref.dtype)
            carry_in = h_end[j:j + 1, :] + p_end[j:j + 1, :] * carry_in


def _lru_mixer(l, bx, bg, cw, cb, wa, ba, wi, bi, lam):
    B, S, C = bx.shape
    chunk = S // SCAN_CHUNKS
    pitch = chunk + SUBLANES
    seq = pl.BlockSpec((None, S, C), lambda b: (b, 0, 0))
    params = (cw, cb, wa, ba, wi, bi, lam)
    return pl.pallas_call(
        functools.partial(_lru_body, pitch=pitch),
        grid=(B,),
        in_specs=[seq, seq] + [_layer(p, l) for p in params],
        out_specs=seq,
        out_shape=jax.ShapeDtypeStruct((B, S, C), BF16),
        scratch_shapes=[pltpu.VMEM((C // LANES, SCAN_CHUNKS * pitch, LANES), F32),
                        pltpu.VMEM((C // LANES, SCAN_CHUNKS * pitch, LANES), F32)],
        compiler_params=_params("arbitrary"),
        name="lru_mixer",
    )(bx, bg, *params)


def _attn_bias_tables(seq):
    blk = ATT_BLOCK
    local = [(w, d) for w, d in DILATED_PATTERNS if d == 1]
    strided = [(w, d) for w, d in DILATED_PATTERNS if d != 1]
    for window, dil in DILATED_PATTERNS:
        n = window // dil
        assert seq % dil == 0 and (seq // dil) % n == 0, "pattern padding not supported"
    assert all(w <= blk for w, _ in local) and all(d % ATT_PERM == 0 for _, d in strided)
    assert seq % (ATT_PERM * blk) == 0

    def bias(mult):
        return jnp.asarray(np.where(mult > 0, np.log2(np.maximum(mult, 1)), NEG_BIAS), F32)

    r = np.arange(blk)[:, None]
    d = r - np.arange(2 * blk)[None, :] + blk
    near = sum(((d >= 0) & (d <= w)).astype(np.int64) for w, _ in local)
    n_sub = seq // ATT_PERM // blk
    d = ATT_PERM * (r - np.arange(n_sub * blk)[None, :] + (n_sub - 1) * blk)
    far = sum(((d >= 0) & (d % dil == 0) & (d <= w)).astype(np.int64) for w, dil in strided)
    return bias(near), bias(far)


def _scores(q, k):
    return lax.dot_general(q, k, (((1,), (1,)), ((), ())), preferred_element_type=F32)


def _attn_head_pair(q_ref, k_ref, v_ref, qp_ref, kp_ref, vp_ref, near_ref, far_ref, o_ref,
                    vn, vp, acc_far, max_far):
    S = q_ref.shape[0]
    blk = ATT_BLOCK
    n_res, sub_len = qp_ref.shape[0], qp_ref.shape[1]
    n_sub = sub_len // blk
    head0 = lax.broadcasted_iota(jnp.int32, (blk, LANES), 1) < HEAD_DIM
    head0_sub = lax.broadcasted_iota(jnp.int32, (sub_len, LANES), 1) < HEAD_DIM
    one = jnp.ones((sub_len, LANES), BF16)
    for h in range(2):
        own = head0_sub if h == 0 else jnp.logical_not(head0_sub)
        for r in range(n_res):
            vp[h, r] = jnp.where(own, vp_ref[r], one)
        for c in range(S // sub_len):
            rows = slice(c * sub_len, (c + 1) * sub_len)
            vn[h, rows, :] = jnp.where(own, v_ref[rows, :], one)

    def own_lanes(h):
        return head0 if h == 0 else jnp.logical_not(head0)

    def far_block(h, r, a):
        nk = (a + 1) * blk

        def scores():
            q = qp_ref[r, a * blk:(a + 1) * blk, :]
            q = jnp.where(own_lanes(h), q, jnp.zeros_like(q))
            return _scores(q, kp_ref[r, 0:nk, :]) + far_ref[:, (n_sub - 1 - a) * blk:]

        def finish(s):
            m = jnp.max(s, axis=-1, keepdims=True)
            p = jnp.exp2(s - m).astype(BF16)
            rows = pl.ds(r + n_res * a * blk, blk, stride=n_res)
            acc_far[h, rows, :] = jnp.dot(p, vp[h, r, 0:nk, :], preferred_element_type=F32)
            max_far[h, rows, :] = jnp.broadcast_to(m, (blk, LANES))

        return scores, finish

    def near_block(i):
        rows = slice(i * blk, (i + 1) * blk)
        keys = slice(max(i - 1, 0) * blk, (i + 1) * blk)
        bias = near_ref[...] if i > 0 else near_ref[:, blk:]

        def scores():
            q = q_ref[rows, :]
            return [_scores(jnp.where(own_lanes(h), q, jnp.zeros_like(q)), k_ref[keys, :]) + bias
                    for h in range(2)]

        def finish(s_heads):
            acc = []
            for h, s in enumerate(s_heads):
                m_far = max_far[h, rows, :]
                m = jnp.maximum(jnp.max(s, axis=-1, keepdims=True), m_far)
                m_keys = m if i == 0 else jnp.concatenate([m, m], axis=-1)
                p = jnp.exp2(s - m_keys).astype(BF16)
                acc.append(jnp.dot(p, vn[h, keys, :], preferred_element_type=F32)
                           + jnp.exp2(m_far - m) * acc_far[h, rows, :])
            num = jnp.where(head0, acc[0], acc[1])
            den = pltpu.roll(jnp.where(head0, acc[1], acc[0]), HEAD_DIM, axis=1)
            o_ref[rows, :] = (num / den).astype(o_ref.dtype)

        return scores, finish

    blocks = []
    for a in range(n_sub):
        blocks += [far_block(h, r, a) for r in range(n_res) for h in range(2)]
        blocks += [near_block(i) for i in range(a * n_res, (a + 1) * n_res)]
    return blocks


def _attn_body(q_ref, k_ref, v_ref, qp_ref, kp_ref, vp_ref, near_ref, far_ref, o_ref,
               vn, vp, acc_far, max_far):
    blocks = []
    for p in range(q_ref.shape[-1] // LANES):
        lanes = slice(p * LANES, (p + 1) * LANES)
        blocks += _attn_head_pair(
            q_ref.at[:, lanes], k_ref.at[:, lanes], v_ref.at[:, lanes],
            qp_ref.at[:, :, lanes], kp_ref.at[:, :, lanes], vp_ref.at[:, :, lanes],
            near_ref, far_ref, o_ref.at[:, lanes],
            vn.at[p], vp.at[p], acc_far.at[p], max_far.at[p])
    pending = []
    for scores, finish in blocks:
        pending.append((finish, scores()))
        if len(pending) > ATT_PIPELINE:
            fn, s = pending.pop(0)
            fn(s)
    for fn, s in pending:
        fn(s)


def _dilated_attention(q, k, v, qp, kp, vp, near, far):
    B, S, W = q.shape
    n_res, sub_len = qp.shape[1], qp.shape[2]
    pairs = ATT_PAIRS_PER_STEP
    width = pairs * LANES
    nat = pl.BlockSpec((None, S, width), lambda b, h: (b, 0, h))
    perm = pl.BlockSpec((None, n_res, sub_len, width), lambda b, h: (b, 0, 0, h))
    return pl.pallas_call(
        _attn_body,
        grid=(B, W // width),
        in_specs=[nat, nat, nat, perm, perm, perm, _full(near.shape), _full(far.shape)],
        out_specs=nat,
        out_shape=jax.ShapeDtypeStruct((B, S, W), BF16),
        scratch_shapes=[pltpu.VMEM((pairs, 2, S, LANES), BF16),
                        pltpu.VMEM((pairs, 2, n_res, sub_len, LANES), BF16),
                        pltpu.VMEM((pairs, 2, S, LANES), F32),
                        pltpu.VMEM((pairs, 2, S, LANES), F32)],
        compiler_params=_params("arbitrary", "arbitrary"),
        name="dilated_attn",
    )(q, k, v, qp, kp, vp, near, far)


def _norm_matmul_body(x_ref, g_ref, w32_ref, o_ref, w_ref):
    @pl.when(pl.program_id(1) == 0)
    def _():
        w_ref[...] = w32_ref[...].astype(BF16)

    hb = _rms(x_ref[...], g_ref[...]).astype(BF16)
    o_ref[...] = jnp.dot(hb, w_ref[...], preferred_element_type=F32).astype(o_ref.dtype)


def _mem_kv_proj(x, g, w):
    T, D = x.shape
    depth, _, N = w.shape
    tm = ROW_TILE
    per_layer = lambda p: pl.BlockSpec((None,) + p.shape[1:], lambda l, i: (l, 0, 0))
    return pl.pallas_call(
        _norm_matmul_body,
        grid=(depth, T // tm),
        in_specs=[pl.BlockSpec((tm, D), lambda l, i: (i, 0)), per_layer(g), per_layer(w)],
        out_specs=pl.BlockSpec((None, tm, N), lambda l, i: (l, i, 0)),
        out_shape=jax.ShapeDtypeStruct((depth, T, N), BF16),
        scratch_shapes=[_mxu_copy(w)],
        compiler_params=_params("arbitrary", "arbitrary"),
        name="mem_kv_proj",
    )(x, g, w)


def _mix_out_mem_body(ya_ref, yb_ref, yc_ref, x_ref, wout32_ref, gmix_ref, gpre_ref, wq32_ref, kv_ref,
                      wo32_ref, gpost_ref, o_ref, wout_ref, wq_ref, wo_ref):
    _cast_weights_once(2, [(wout32_ref, wout_ref), (wq32_ref, wq_ref), (wo32_ref, wo_ref)])
    tm, D = x_ref.shape
    hd = D // MEM_HEADS

    def stages(rows):
        y = jnp.concatenate([ya_ref[rows, :], yb_ref[rows, :], yc_ref[rows, :]], axis=-1)
        y = jnp.dot(y, wout_ref[...], preferred_element_type=F32)
        yield
        x = x_ref[rows, :] + _rms(y, gmix_ref[...])
        hb = _rms(x, gpre_ref[...]).astype(BF16)
        yield
        q = jnp.dot(hb, wq_ref[...], preferred_element_type=F32) * (hd ** -0.5 * LOG2E)
        q = q.astype(BF16)
        scores = [_scores(q[:, h * hd:(h + 1) * hd], kv_ref[:, h * hd:(h + 1) * hd])
                  for h in range(MEM_HEADS)]
        yield
        probs = []
        for s in scores:
            p = jnp.exp2(s - jnp.max(s, axis=-1, keepdims=True))
            probs.append((p.astype(BF16), jnp.sum(p, axis=-1, keepdims=True)))
        yield
        outs = []
        for h, (p, den) in enumerate(probs):
            vh = kv_ref[:, D + h * hd:D + (h + 1) * hd]
            outs.append((jnp.dot(p, vh, preferred_element_type=F32) / den).astype(BF16))
        y = jnp.dot(jnp.concatenate(outs, axis=-1), wo_ref[...], preferred_element_type=F32)
        yield
        o_ref[rows, :] = x + _rms(y, gpost_ref[...])

    half = tm // MIX_ROW_GROUPS
    chains = [stages(slice(g * half, (g + 1) * half)) for g in range(MIX_ROW_GROUPS)]
    while chains:
        chains = [c for c in chains if next(c, StopIteration) is not StopIteration]


def _mix_out_mem(l, ya, yb, yc, x, wout, gmix, gpre, wq, kv, wo, gpost):
    B, S, D = x.shape
    M = kv.shape[2]
    tm = MIX_ROW_TILE
    row = lambda n: pl.BlockSpec((None, tm, n), lambda b, s: (b, s, 0))
    return pl.pallas_call(
        _mix_out_mem_body,
        grid=(B, S // tm),
        in_specs=[row(ya.shape[-1]), row(yb.shape[-1]), row(yc.shape[-1]), row(D),
                  _layer(wout, l), _layer(gmix, l), _layer(gpre, l), _layer(wq, l),
                  pl.BlockSpec((None, None, M, 2 * D), lambda b, s: (l, b, 0, 0)),
                  _layer(wo, l), _layer(gpost, l)],
        out_specs=row(D),
        out_shape=jax.ShapeDtypeStruct((B, S, D), F32),
        scratch_shapes=[_mxu_copy(wout), _mxu_copy(wq), _mxu_copy(wo)],
        compiler_params=_params("arbitrary", "arbitrary"),
        name="mix_out_mem_attn",
    )(ya, yb, yc, x, wout, gmix, gpre, wq, kv, wo, gpost)


def _ffn_body(x_ref, gpre_ref, wup_ref, cw_ref, cb_ref, wd_ref, gpost_ref, o_ref,
              carry, act, *, d_ff, tf):
    tm = x_ref.shape[0]
    pad = SUBLANES
    n_chunks = d_ff // tf
    first_rows = lax.broadcasted_iota(jnp.int32, (pad, tf), 0)

    @pl.when(pl.program_id(1) == 0)
    def _():
        carry[...] = jnp.zeros(carry.shape, F32)

    x = x_ref[...]
    hb = _rms(x, gpre_ref[...]).astype(BF16)

    def conv(col0):
        cols = slice(col0, col0 + tf)
        u = jnp.dot(hb, wup_ref[:, cols], preferred_element_type=F32)
        last = carry[:, cols]
        carry[:, cols] = u[tm - pad:tm, :]
        w = cw_ref[:, cols]
        y = cb_ref[:, cols] + w[FFN_CONV_KERNEL - 1:FFN_CONV_KERNEL, :] * u
        for t in range(FFN_CONV_KERNEL - 1):
            back = FFN_CONV_KERNEL - 1 - t
            shifted = pltpu.roll(u, back, axis=0)
            head = jnp.where(first_rows < back, pltpu.roll(last, back, axis=0), shifted[0:pad, :])
            y = y + w[t:t + 1, :] * jnp.concatenate([head, shifted[pad:, :]], axis=0)
        return y

    for c in range(n_chunks):
        gate = conv(c * tf)
        up = conv(d_ff + c * tf)
        act[:, c * tf:(c + 1) * tf] = (_gelu_tanh(gate) * up).astype(BF16)
    y = jnp.dot(act[...], wd_ref[...], preferred_element_type=F32)
    o_ref[...] = x + _rms(y, gpost_ref[...])


def _ffn(l, x, gpre, wup, cw, cb, wd, gpost):
    B, S, D = x.shape
    d_ff = wd.shape[1]
    tm = FFN_ROW_TILE
    tf = MXU_DIM
    row = pl.BlockSpec((None, tm, D), lambda b, s: (b, s, 0))
    return pl.pallas_call(
        functools.partial(_ffn_body, d_ff=d_ff, tf=tf),
        grid=(B, S // tm),
        in_specs=[row, _layer(gpre, l), _layer(wup, 0), _layer(cw, l), _layer(cb, l),
                  _layer(wd, 0), _layer(gpost, l)],
        out_specs=row,
        out_shape=jax.ShapeDtypeStruct((B, S, D), F32),
        scratch_shapes=[pltpu.VMEM((SUBLANES, 2 * d_ff), F32),
                        pltpu.VMEM((tm, d_ff), BF16)],
        compiler_params=_params("arbitrary", "arbitrary"),
        name="conv_ffn",
    )(x, gpre, wup, cw, cb, wd, gpost)


def _rope_tables(seq, heads):
    inv = 1.0 / (ROPE_THETA ** (jnp.arange(0, HEAD_DIM, 2, dtype=F32) / HEAD_DIM))
    ang = jnp.arange(seq, dtype=F32)[:, None] * inv[None, :]
    cos, sin = jnp.cos(ang), jnp.sin(ang)
    cos_h = jnp.concatenate([cos, cos], axis=-1)
    sin_h = jnp.concatenate([-sin, sin], axis=-1)
    return jnp.tile(cos_h, (1, heads)), jnp.tile(sin_h, (1, heads))


def _block_diag(w):
    depth, h, n, _ = w.shape
    eye = jnp.eye(h, dtype=w.dtype)
    return jnp.einsum('lhij,hg->lhigj', w, eye).reshape(depth, h * n, h * n)


def kernel(x, mem, g_mix_pre, w_in, conv_w, conv_b, conv_ln_g, conv_ln_b, lru_conv_w, lru_conv_b, lru_w_a, lru_b_a, lru_w_i, lru_b_i, lru_lambda, w_out, g_mix_post, g_mem_pre, g_mem_kv, w_mem_q, w_mem_kv, w_mem_o, g_mem_post, g_ffn_pre, w_up, ffn_conv_w, ffn_conv_b, w_down, g_ffn_post):
    B, S, D = x.shape
    depth = w_in.shape[0]
    conv_ch = conv_w.shape[-1]
    lru_w = lru_conv_w.shape[-1]
    att_w = w_out.shape[1] - conv_ch - lru_w
    n_mem = mem.shape[1]
    assert S % ROW_TILE == 0 and ROW_TILE % (ATT_PERM * SUBLANES * 2) == 0
    assert att_w % LANES == 0 and LANES == 2 * HEAD_DIM and (B * n_mem) % ROW_TILE == 0

    cos, sin = _rope_tables(S, att_w // HEAD_DIM)
    near, far = _attn_bias_tables(S)
    vec = lambda p: p.reshape(depth, 1, -1)
    mxu = lambda p: p.astype(BF16)
    mem2 = mem.reshape(B * n_mem, D)

    g_mix_pre, g_mix_post, g_mem_pre, g_mem_kv, g_mem_post, g_ffn_pre, g_ffn_post = map(
        vec, (g_mix_pre, g_mix_post, g_mem_pre, g_mem_kv, g_mem_post, g_ffn_pre, g_ffn_post))
    conv_b, conv_ln_g, conv_ln_b, lru_conv_b, lru_b_a, lru_b_i, lru_lambda, ffn_conv_b = map(
        vec, (conv_b, conv_ln_g, conv_ln_b, lru_conv_b, lru_b_a, lru_b_i, lru_lambda, ffn_conv_b))
    lru_w_a = mxu(_block_diag(lru_w_a))
    lru_w_i = mxu(_block_diag(lru_w_i))
    kv = _mem_kv_proj(mem2, g_mem_kv, w_mem_kv).reshape(depth, B, n_mem, 2 * D)

    for l in range(depth):
        y_a, bx, bg, q, k, v, qp, kp, vp, w_up_l, w_down_l = _in_proj(
            l, x, g_mix_pre, w_in, cos, sin, conv_w, conv_b, conv_ln_g, conv_ln_b, w_up, w_down,
            conv_ch, lru_w, att_w)
        y_b = _lru_mixer(l, bx, bg, lru_conv_w, lru_conv_b, lru_w_a, lru_b_a, lru_w_i, lru_b_i,
                         lru_lambda)
        y_c = _dilated_attention(q, k, v, qp, kp, vp, near, far)
        x = _mix_out_mem(l, y_a, y_b, y_c, x, w_out, g_mix_post, g_mem_pre, w_mem_q, kv, w_mem_o,
                         g_mem_post)
        x = _ffn(l, x, g_ffn_pre, w_up_l, ffn_conv_w, ffn_conv_b, w_down_l, g_ffn_post)
    return x
```

```python
import functools
import math

import numpy as np
import jax
import jax.numpy as jnp
from jax import lax
from jax.experimental import pallas as pl
from jax.experimental.pallas import tpu as pltpu

F32 = jnp.float32
BF16 = jnp.bfloat16

HEAD_DIM = 64
CONV_KERNEL = 31
LRU_CONV_KERNEL = 4
LRU_C = 8.0
DILATED_PATTERNS = ((128, 1), (512, 4), (2048, 16))
ROPE_THETA = 10000.0
MEM_HEADS = 4
FFN_CONV_KERNEL = 3
EPS = 1e-6

LANES = 128
SUBLANES = 8
MXU_DIM = 256
VMEM_LIMIT_BYTES = 56 * 1024 * 1024

ROW_TILE = 512
FFN_ROW_TILE = 1024
MIX_ROW_TILE = 1024
ATT_BLOCK = 128
ATT_PERM = 4
ATT_PAIRS_PER_STEP = 2
ATT_PIPELINE = 4
CONV_HALO = 32
CONV_SUB = 64
MIX_ROW_GROUPS = 4
LOG2E = 1.4426950408889634
SCAN_CHUNKS = SUBLANES
NEG_BIAS = -1e30


def _params(*sem):
    return pltpu.CompilerParams(dimension_semantics=sem, vmem_limit_bytes=VMEM_LIMIT_BYTES)


def _rms(x, g):
    return x * lax.rsqrt(jnp.mean(x * x, axis=-1, keepdims=True) + EPS) * g


def _sigmoid(x):
    return 1.0 / (1.0 + jnp.exp2(x * (-LOG2E)))


def _gelu_tanh(x):
    k = -2.0 * math.sqrt(2.0 / math.pi) * LOG2E
    return x / (1.0 + jnp.exp2(x * (k + (0.044715 * k) * (x * x))))


def _full(shape):
    n = len(shape)
    return pl.BlockSpec(shape, lambda *_: (0,) * n)


def _layer(p, l):
    return pl.BlockSpec((None,) + p.shape[1:], lambda *_: (l, 0, 0), pipeline_mode=pl.Buffered(1))


def _cast_weights_once(grid_rank, pairs):
    first = pl.program_id(0) == 0
    for axis in range(1, grid_rank):
        first = jnp.logical_and(first, pl.program_id(axis) == 0)

    @pl.when(first)
    def _():
        for src, dst in pairs:
            dst[...] = src[...].astype(dst.dtype)


def _mxu_copy(w):
    return pltpu.VMEM(w.shape[1:], BF16)


def _rope(t, cos, sin_signed, first_half):
    n = t.shape[-1]
    half = HEAD_DIM // 2
    partner = jnp.where(first_half, pltpu.roll(t, n - half, axis=1), pltpu.roll(t, half, axis=1))
    return t * cos + partner * sin_signed


def _store_attn_operand(t, nat_ref, perm_ref, stage):
    rows, width = t.shape
    nat_ref[...] = t.astype(BF16)
    for c in range(width // LANES):
        stage[c] = t[:, c * LANES:(c + 1) * LANES]
    for c in range(width // LANES):
        for r in range(ATT_PERM):
            sub = stage[c, pl.ds(r, rows // ATT_PERM, stride=ATT_PERM), :]
            perm_ref[r, :, c * LANES:(c + 1) * LANES] = sub.astype(BF16)


def _conv_mixer_halo(ubuf, tm):
    @pl.when(pl.program_id(1) == 0)
    def _():
        ubuf[0:CONV_HALO, :] = jnp.zeros((CONV_HALO, ubuf.shape[-1]), F32)

    @pl.when(pl.program_id(1) > 0)
    def _():
        ubuf[0:CONV_HALO, :] = ubuf[tm:tm + CONV_HALO, :]


def _conv_mixer_prepare(a, ubuf, ush, *, ch):
    tm = a.shape[0]
    ubuf[CONV_HALO:CONV_HALO + tm, :] = a[:, :ch] * _sigmoid(a[:, ch:])
    span = CONV_HALO + tm - SUBLANES
    for s in range(1, SUBLANES):
        ush[s - 1, 0:span, :] = ubuf[s:s + span, :]


def _conv_mixer_rows(r0, w_ref, b_ref, g_ref, beta_ref, o_ref, ubuf, ush):
    ch = ubuf.shape[-1]
    w = w_ref[...]
    off = CONV_HALO - (CONV_KERNEL - 1)
    acc = jnp.zeros((CONV_SUB, ch), F32) + b_ref[...]
    for j in range(CONV_KERNEL):
        s = (off + j) % SUBLANES
        lo = r0 + off + j - s
        win = ubuf[lo:lo + CONV_SUB, :] if s == 0 else ush[s - 1, lo:lo + CONV_SUB, :]
        acc = acc + w[j:j + 1, :] * win
    mu = jnp.mean(acc, axis=-1, keepdims=True)
    cen = acc - mu
    var = jnp.mean(cen * cen, axis=-1, keepdims=True)
    y = cen * lax.rsqrt(var + EPS) * g_ref[...] + beta_ref[...]
    o_ref[r0:r0 + CONV_SUB, :] = (y * _sigmoid(y)).astype(o_ref.dtype)


def _in_proj_body(x_ref, g_ref, w32_ref, cos_ref, sin_ref, cw_ref, cb_ref, cg_ref, cbeta_ref,
                  wup32_ref, wdown32_ref,
                  ya_ref, lru_ref, qkv_ref, qkvp_ref, wup_ref, wdown_ref,
                  w_ref, ubuf, ush, stage_q, stage_k, stage_v, *, splits, n_down_pieces):
    tm = x_ref.shape[0]
    _cast_weights_once(2, [(w32_ref, w_ref)])
    step = pl.program_id(0) * pl.num_programs(1) + pl.program_id(1)
    wup_ref[...] = wup32_ref[...].astype(BF16)

    @pl.when(step < n_down_pieces)
    def _():
        wdown_ref[...] = wdown32_ref[...].astype(BF16)

    _conv_mixer_halo(ubuf, tm)
    hb = _rms(x_ref[...], g_ref[...]).astype(BF16)

    def proj(lo, hi):
        return jnp.dot(hb, w_ref[:, lo:hi], preferred_element_type=F32)

    c_a, c_bx, c_bg, c_q, c_k, c_v = splits
    rows = pl.ds(pl.multiple_of(pl.program_id(1) * tm, tm), tm)
    cos = cos_ref[rows, :]
    sin = sin_ref[rows, :]
    lane = lax.broadcasted_iota(jnp.int32, cos.shape, 1)
    first_half = (lane % HEAD_DIM) < (HEAD_DIM // 2)
    _conv_mixer_prepare(proj(0, c_a), ubuf, ush, ch=c_a // 2)
    for r0 in range(0, tm, CONV_SUB):
        _conv_mixer_rows(r0, cw_ref, cb_ref, cg_ref, cbeta_ref, ya_ref, ubuf, ush)
    lru_ref[...] = proj(c_a, c_bg)
    att_w = c_q - c_bg
    cols = [slice(i * att_w, (i + 1) * att_w) for i in range(3)]
    q = _rope(proj(c_bg, c_q), cos, sin, first_half) * (HEAD_DIM ** -0.5 * LOG2E)
    _store_attn_operand(q, qkv_ref.at[:, cols[0]], qkvp_ref.at[:, :, cols[0]], stage_q)
    _store_attn_operand(_rope(proj(c_q, c_k), cos, sin, first_half),
                        qkv_ref.at[:, cols[1]], qkvp_ref.at[:, :, cols[1]], stage_k)
    _store_attn_operand(proj(c_k, c_v), qkv_ref.at[:, cols[2]], qkvp_ref.at[:, :, cols[2]], stage_v)


def _in_proj(l, x, g, w, cos, sin, conv_w, conv_b, conv_g, conv_beta, w_up, w_down,
             conv_ch, lru_w, att_w):
    B, S, D = x.shape
    tm = ROW_TILE
    n_steps = B * (S // tm)
    bf16_rows = 2 * SUBLANES
    up_rows = w_up.shape[1] // n_steps
    down_rows = LANES
    n_down = w_down.shape[1] // down_rows
    assert up_rows * n_steps == w_up.shape[1] and up_rows % bf16_rows == 0
    assert n_down * down_rows == w_down.shape[1] and n_down <= n_steps
    steps_per_b = S // tm
    up_piece = lambda b, s: (l, b * steps_per_b + s, 0)
    down_piece = lambda b, s: (l, jnp.minimum(b * steps_per_b + s, n_down - 1), 0)
    splits = (2 * conv_ch, 2 * conv_ch + lru_w, 2 * conv_ch + 2 * lru_w,
              2 * conv_ch + 2 * lru_w + att_w, 2 * conv_ch + 2 * lru_w + 2 * att_w,
              2 * conv_ch + 2 * lru_w + 3 * att_w)
    row = lambda n: pl.BlockSpec((None, tm, n), lambda b, s: (b, s, 0))
    perm = pl.BlockSpec((None, ATT_PERM, tm // ATT_PERM, 3 * att_w), lambda b, s: (b, 0, s, 0))
    stage = pltpu.VMEM((att_w // LANES, tm, LANES), F32)
    return pl.pallas_call(
        functools.partial(_in_proj_body, splits=splits, n_down_pieces=n_down),
        grid=(B, S // tm),
        in_specs=[row(D), _layer(g, l), _layer(w, l), _full(cos.shape), _full(sin.shape),
                  _layer(conv_w, l), _layer(conv_b, l), _layer(conv_g, l), _layer(conv_beta, l),
                  pl.BlockSpec((None, up_rows, w_up.shape[2]), up_piece),
                  pl.BlockSpec((None, down_rows, w_down.shape[2]), down_piece)],
        out_specs=[row(conv_ch), row(2 * lru_w), row(3 * att_w), perm,
                   pl.BlockSpec((None, up_rows, w_up.shape[2]), lambda b, s: (0,) + up_piece(b, s)[1:]),
                   pl.BlockSpec((None, down_rows, w_down.shape[2]),
                                lambda b, s: (0,) + down_piece(b, s)[1:])],
        out_shape=[jax.ShapeDtypeStruct((B, S, conv_ch), BF16),
                   jax.ShapeDtypeStruct((B, S, 2 * lru_w), F32),
                   jax.ShapeDtypeStruct((B, S, 3 * att_w), BF16),
                   jax.ShapeDtypeStruct((B, ATT_PERM, S // ATT_PERM, 3 * att_w), BF16),
                   jax.ShapeDtypeStruct((1,) + w_up.shape[1:], BF16),
                   jax.ShapeDtypeStruct((1,) + w_down.shape[1:], BF16)],
        scratch_shapes=[_mxu_copy(w),
                        pltpu.VMEM((CONV_HALO + tm, conv_ch), F32),
                        pltpu.VMEM((SUBLANES - 1, CONV_HALO + tm - SUBLANES, conv_ch), F32),
                        stage, stage, stage],
        compiler_params=_params("arbitrary", "arbitrary"),
        name="in_proj",
    )(x, g, w, cos, sin, conv_w, conv_b, conv_g, conv_beta, w_up, w_down)


def _lru_body(bx_ref, bg_ref, cw_ref, cb_ref, wa_ref, ba_ref, wi_ref, bi_ref, lam_ref, o_ref,
              a_s, b_s, *, pitch):
    S, C = bx_ref.shape
    n_slab = C // LANES
    chunk = S // SCAN_CHUNKS
    pad = SUBLANES
    first_rows = lax.broadcasted_iota(jnp.int32, (pad, C), 0)
    z = -lam_ref[...]
    log_a_scale = (-LRU_C) * (jnp.maximum(z, 0.0) + jnp.log1p(jnp.exp(-jnp.abs(z))))
    cw = cw_ref[...]
    for j in range(SCAN_CHUNKS):
        base = j * chunk
        xc = bx_ref[base:base + chunk, :]
        last = bx_ref[base - pad:base, :] if j > 0 else jnp.zeros((pad, C), F32)
        u = cb_ref[...] + cw[LRU_CONV_KERNEL - 1:LRU_CONV_KERNEL, :] * xc
        for t in range(LRU_CONV_KERNEL - 1):
            back = LRU_CONV_KERNEL - 1 - t
            shifted = pltpu.roll(xc, back, axis=0)
            head = jnp.where(first_rows < back, pltpu.roll(last, back, axis=0), shifted[0:pad, :])
            u = u + cw[t:t + 1, :] * jnp.concatenate([head, shifted[pad:, :]], axis=0)
        ub = u.astype(BF16)
        r = _sigmoid(jnp.dot(ub, wa_ref[...], preferred_element_type=F32) + ba_ref[...])
        gate_i = _sigmoid(jnp.dot(ub, wi_ref[...], preferred_element_type=F32) + bi_ref[...])
        log_a = r * log_a_scale
        a = jnp.exp(log_a)
        bterm = jnp.sqrt(jnp.tanh(-log_a) * (1.0 + a * a)) * (gate_i * u)
        for s in range(n_slab):
            a_s[s, j * pitch:j * pitch + chunk, :] = a[:, s * LANES:(s + 1) * LANES]
            b_s[s, j * pitch:j * pitch + chunk, :] = bterm[:, s * LANES:(s + 1) * LANES]

    def step(i, carry):
        out = []
        for s in range(n_slab):
            h, p = carry[2 * s], carry[2 * s + 1]
            idx = pl.ds(i, SCAN_CHUNKS, stride=pitch)
            av = a_s[s, idx, :]
            h = av * h + b_s[s, idx, :]
            p = av * p
            b_s[s, idx, :] = h
            a_s[s, idx, :] = p
            out += [h, p]
        return tuple(out)

    init = []
    for s in range(n_slab):
        init += [jnp.zeros((SCAN_CHUNKS, LANES), F32), jnp.ones((SCAN_CHUNKS, LANES), F32)]
    fin = lax.fori_loop(0, chunk, step, tuple(init), unroll=8)

    for s in range(n_slab):
        h_end, p_end = fin[2 * s], fin[2 * s + 1]
        carry_in = jnp.zeros((1, LANES), F32)
        for j in range(SCAN_CHUNKS):
            rows = slice(j * pitch, j * pitch + chunk)
            h = b_s[s, rows, :] + a_s[s, rows, :] * carry_in
            gate = _gelu_tanh(bg_ref[j * chunk:(j + 1) * chunk, s * LANES:(s + 1) * LANES])
            o_ref[j * chunk:(j + 1) * chunk, s * LANES:(s + 1) * LANES] = (h * gate).astype(o_ref.dtype)
            carry_in = h_end[j:j + 1, :] + p_end[j:j + 1, :] * carry_in


def _lru_mixer(l, b_xg, cw, cb, wa, ba, wi, bi, lam):
    B, S, C = b_xg.shape[0], b_xg.shape[1], b_xg.shape[2] // 2
    chunk = S // SCAN_CHUNKS
    pitch = chunk + SUBLANES
    seq = pl.BlockSpec((None, S, C), lambda b: (b, 0, 0))
    gate = pl.BlockSpec((None, S, C), lambda b: (b, 0, 1))
    params = (cw, cb, wa, ba, wi, bi, lam)
    return pl.pallas_call(
        functools.partial(_lru_body, pitch=pitch),
        grid=(B,),
        in_specs=[seq, gate] + [_layer(p, l) for p in params],
        out_specs=seq,
        out_shape=jax.ShapeDtypeStruct((B, S, C), BF16),
        scratch_shapes=[pltpu.VMEM((C // LANES, SCAN_CHUNKS * pitch, LANES), F32),
                        pltpu.VMEM((C // LANES, SCAN_CHUNKS * pitch, LANES), F32)],
        compiler_params=_params("arbitrary"),
        name="lru_mixer",
    )(b_xg, b_xg, *params)


def _attn_bias_tables(seq):
    blk = ATT_BLOCK
    local = [(w, d) for w, d in DILATED_PATTERNS if d == 1]
    strided = [(w, d) for w, d in DILATED_PATTERNS if d != 1]
    for window, dil in DILATED_PATTERNS:
        n = window // dil
        assert seq % dil == 0 and (seq // dil) % n == 0, "pattern padding not supported"
    assert all(w <= blk for w, _ in local) and all(d % ATT_PERM == 0 for _, d in strided)
    assert seq % (ATT_PERM * blk) == 0

    def bias(mult):
        return jnp.asarray(np.where(mult > 0, np.log2(np.maximum(mult, 1)), NEG_BIAS), F32)

    r = np.arange(blk)[:, None]
    d = r - np.arange(2 * blk)[None, :] + blk
    near = sum(((d >= 0) & (d <= w)).astype(np.int64) for w, _ in local)
    n_sub = seq // ATT_PERM // blk
    d = ATT_PERM * (r - np.arange(n_sub * blk)[None, :] + (n_sub - 1) * blk)
    far = sum(((d >= 0) & (d % dil == 0) & (d <= w)).astype(np.int64) for w, dil in strided)
    return bias(near), bias(far)


def _scores(q, k):
    return lax.dot_general(q, k, (((1,), (1,)), ((), ())), preferred_element_type=F32)


def _attn_head_pair(q_ref, k_ref, v_ref, qp_ref, kp_ref, vp_ref, near_ref, far_ref, o_ref,
                    vn, vp, acc_far, max_far):
    S = q_ref.shape[0]
    blk = ATT_BLOCK
    n_res, sub_len = qp_ref.shape[0], qp_ref.shape[1]
    n_sub = sub_len // blk
    head0 = lax.broadcasted_iota(jnp.int32, (blk, LANES), 1) < HEAD_DIM
    head0_sub = lax.broadcasted_iota(jnp.int32, (sub_len, LANES), 1) < HEAD_DIM
    one = jnp.ones((sub_len, LANES), BF16)
    for h in range(2):
        own = head0_sub if h == 0 else jnp.logical_not(head0_sub)
        for r in range(n_res):
            vp[h, r] = jnp.where(own, vp_ref[r], one)
        for c in range(S // sub_len):
            rows = slice(c * sub_len, (c + 1) * sub_len)
            vn[h, rows, :] = jnp.where(own, v_ref[rows, :], one)

    def own_lanes(h):
        return head0 if h == 0 else jnp.logical_not(head0)

    def far_block(h, r, a):
        nk = (a + 1) * blk

        def scores():
            q = qp_ref[r, a * blk:(a + 1) * blk, :]
            q = jnp.where(own_lanes(h), q, jnp.zeros_like(q))
            return _scores(q, kp_ref[r, 0:nk, :]) + far_ref[:, (n_sub - 1 - a) * blk:]

        def finish(s):
            m = jnp.max(s, axis=-1, keepdims=True)
            p = jnp.exp2(s - m).astype(BF16)
            rows = pl.ds(r + n_res * a * blk, blk, stride=n_res)
            acc_far[h, rows, :] = jnp.dot(p, vp[h, r, 0:nk, :], preferred_element_type=F32)
            max_far[h, rows, :] = jnp.broadcast_to(m, (blk, LANES))

        return scores, finish

    def near_block(i):
        rows = slice(i * blk, (i + 1) * blk)
        keys = slice(max(i - 1, 0) * blk, (i + 1) * blk)
        bias = near_ref[...] if i > 0 else near_ref[:, blk:]

        def scores():
            q = q_ref[rows, :]
            return [_scores(jnp.where(own_lanes(h), q, jnp.zeros_like(q)), k_ref[keys, :]) + bias
                    for h in range(2)]

        def finish(s_heads):
            acc = []
            for h, s in enumerate(s_heads):
                m_far = max_far[h, rows, :]
                m = jnp.maximum(jnp.max(s, axis=-1, keepdims=True), m_far)
                m_keys = m if i == 0 else jnp.concatenate([m, m], axis=-1)
                p = jnp.exp2(s - m_keys).astype(BF16)
                acc.append(jnp.dot(p, vn[h, keys, :], preferred_element_type=F32)
                           + jnp.exp2(m_far - m) * acc_far[h, rows, :])
            num = jnp.where(head0, acc[0], acc[1])
            den = pltpu.roll(jnp.where(head0, acc[1], acc[0]), HEAD_DIM, axis=1)
            o_ref[rows, :] = (num / den).astype(o_ref.dtype)

        return scores, finish

    blocks = []
    for a in range(n_sub):
        blocks += [far_block(h, r, a) for r in range(n_res) for h in range(2)]
        blocks += [near_block(i) for i in range(a * n_res, (a + 1) * n_res)]
    return blocks


def _attn_body(q_ref, k_ref, v_ref, qp_ref, kp_ref, vp_ref, near_ref, far_ref, o_ref,
               vn, vp, acc_far, max_far):
    blocks = []
    for p in range(q_ref.shape[-1] // LANES):
        lanes = slice(p * LANES, (p + 1) * LANES)
        blocks += _attn_head_pair(
            q_ref.at[:, lanes], k_ref.at[:, lanes], v_ref.at[:, lanes],
            qp_ref.at[:, :, lanes], kp_ref.at[:, :, lanes], vp_ref.at[:, :, lanes],
            near_ref, far_ref, o_ref.at[:, lanes],
            vn.at[p], vp.at[p], acc_far.at[p], max_far.at[p])
    pending = []
    for scores, finish in blocks:
        pending.append((finish, scores()))
        if len(pending) > ATT_PIPELINE:
            fn, s = pending.pop(0)
            fn(s)
    for fn, s in pending:
        fn(s)


def _dilated_attention(qkv, qkvp, near, far):
    B, S, W = qkv.shape[0], qkv.shape[1], qkv.shape[2] // 3
    n_res, sub_len = qkvp.shape[1], qkvp.shape[2]
    pairs = ATT_PAIRS_PER_STEP
    width = pairs * LANES
    steps = W // width
    nat = lambda part: pl.BlockSpec((None, S, width), lambda b, h: (b, 0, part * steps + h))
    perm = lambda part: pl.BlockSpec((None, n_res, sub_len, width),
                                     lambda b, h: (b, 0, 0, part * steps + h))
    return pl.pallas_call(
        _attn_body,
        grid=(B, steps),
        in_specs=[nat(0), nat(1), nat(2), perm(0), perm(1), perm(2),
                  _full(near.shape), _full(far.shape)],
        out_specs=nat(0),
        out_shape=jax.ShapeDtypeStruct((B, S, W), BF16),
        scratch_shapes=[pltpu.VMEM((pairs, 2, S, LANES), BF16),
                        pltpu.VMEM((pairs, 2, n_res, sub_len, LANES), BF16),
                        pltpu.VMEM((pairs, 2, S, LANES), F32),
                        pltpu.VMEM((pairs, 2, S, LANES), F32)],
        compiler_params=_params("arbitrary", "arbitrary"),
        name="dilated_attn",
    )(qkv, qkv, qkv, qkvp, qkvp, qkvp, near, far)


def _norm_matmul_body(x_ref, g_ref, w32_ref, o_ref, w_ref):
    @pl.when(pl.program_id(1) == 0)
    def _():
        w_ref[...] = w32_ref[...].astype(BF16)

    hb = _rms(x_ref[...], g_ref[...]).astype(BF16)
    o_ref[...] = jnp.dot(hb, w_ref[...], preferred_element_type=F32).astype(o_ref.dtype)


def _mem_kv_proj(x, g, w):
    T, D = x.shape
    depth, _, N = w.shape
    tm = ROW_TILE
    per_layer = lambda p: pl.BlockSpec((None,) + p.shape[1:], lambda l, i: (l, 0, 0))
    return pl.pallas_call(
        _norm_matmul_body,
        grid=(depth, T // tm),
        in_specs=[pl.BlockSpec((tm, D), lambda l, i: (i, 0)), per_layer(g), per_layer(w)],
        out_specs=pl.BlockSpec((None, tm, N), lambda l, i: (l, i, 0)),
        out_shape=jax.ShapeDtypeStruct((depth, T, N), BF16),
        scratch_shapes=[_mxu_copy(w)],
        compiler_params=_params("arbitrary", "arbitrary"),
        name="mem_kv_proj",
    )(x, g, w)


def _mix_out_mem_body(ya_ref, yb_ref, yc_ref, x_ref, wout32_ref, gmix_ref, gpre_ref, wq32_ref, kv_ref,
                      wo32_ref, gpost_ref, o_ref, wout_ref, wq_ref, wo_ref):
    _cast_weights_once(2, [(wout32_ref, wout_ref), (wq32_ref, wq_ref), (wo32_ref, wo_ref)])
    tm, D = x_ref.shape
    hd = D // MEM_HEADS

    def stages(rows):
        y = jnp.concatenate([ya_ref[rows, :], yb_ref[rows, :], yc_ref[rows, :]], axis=-1)
        y = jnp.dot(y, wout_ref[...], preferred_element_type=F32)
        yield
        x = x_ref[rows, :] + _rms(y, gmix_ref[...])
        hb = _rms(x, gpre_ref[...]).astype(BF16)
        yield
        q = jnp.dot(hb, wq_ref[...], preferred_element_type=F32) * (hd ** -0.5 * LOG2E)
        q = q.astype(BF16)
        scores = [_scores(q[:, h * hd:(h + 1) * hd], kv_ref[:, h * hd:(h + 1) * hd])
                  for h in range(MEM_HEADS)]
        yield
        probs = []
        for s in scores:
            p = jnp.exp2(s - jnp.max(s, axis=-1, keepdims=True))
            probs.append((p.astype(BF16), jnp.sum(p, axis=-1, keepdims=True)))
        yield
        outs = []
        for h, (p, den) in enumerate(probs):
            vh = kv_ref[:, D + h * hd:D + (h + 1) * hd]
            outs.append((jnp.dot(p, vh, preferred_element_type=F32) / den).astype(BF16))
        y = jnp.dot(jnp.concatenate(outs, axis=-1), wo_ref[...], preferred_element_type=F32)
        yield
        o_ref[rows, :] = x + _rms(y, gpost_ref[...])

    half = tm // MIX_ROW_GROUPS
    chains = [stages(slice(g * half, (g + 1) * half)) for g in range(MIX_ROW_GROUPS)]
    while chains:
        chains = [c for c in chains if next(c, StopIteration) is not StopIteration]


def _mix_out_mem(l, ya, yb, yc, x, wout, gmix, gpre, wq, kv, wo, gpost):
    B, S, D = x.shape
    M = kv.shape[2]
    tm = MIX_ROW_TILE
    row = lambda n: pl.BlockSpec((None, tm, n), lambda b, s: (b, s, 0))
    return pl.pallas_call(
        _mix_out_mem_body,
        grid=(B, S // tm),
        in_specs=[row(ya.shape[-1]), row(yb.shape[-1]), row(yc.shape[-1]), row(D),
                  _layer(wout, l), _layer(gmix, l), _layer(gpre, l), _layer(wq, l),
                  pl.BlockSpec((None, None, M, 2 * D), lambda b, s: (l, b, 0, 0)),
                  _layer(wo, l), _layer(gpost, l)],
        out_specs=row(D),
        out_shape=jax.ShapeDtypeStruct((B, S, D), F32),
        scratch_shapes=[_mxu_copy(wout), _mxu_copy(wq), _mxu_copy(wo)],
        compiler_params=_params("arbitrary", "arbitrary"),
        name="mix_out_mem_attn",
    )(ya, yb, yc, x, wout, gmix, gpre, wq, kv, wo, gpost)


def _ffn_body(x_ref, gpre_ref, wup_ref, cw_ref, cb_ref, wd_ref, gpost_ref, o_ref,
              carry, act, *, d_ff, tf):
    tm = x_ref.shape[0]
    pad = SUBLANES
    n_chunks = d_ff // tf
    first_rows = lax.broadcasted_iota(jnp.int32, (pad, tf), 0)

    @pl.when(pl.program_id(1) == 0)
    def _():
        carry[...] = jnp.zeros(carry.shape, F32)

    x = x_ref[...]
    hb = _rms(x, gpre_ref[...]).astype(BF16)

    def conv(col0):
        cols = slice(col0, col0 + tf)
        u = jnp.dot(hb, wup_ref[:, cols], preferred_element_type=F32)
        last = carry[:, cols]
        carry[:, cols] = u[tm - pad:tm, :]
        w = cw_ref[:, cols]
        y = cb_ref[:, cols] + w[FFN_CONV_KERNEL - 1:FFN_CONV_KERNEL, :] * u
        for t in range(FFN_CONV_KERNEL - 1):
            back = FFN_CONV_KERNEL - 1 - t
            shifted = pltpu.roll(u, back, axis=0)
            head = jnp.where(first_rows < back, pltpu.roll(last, back, axis=0), shifted[0:pad, :])
            y = y + w[t:t + 1, :] * jnp.concatenate([head, shifted[pad:, :]], axis=0)
        return y

    for c in range(n_chunks):
        gate = conv(c * tf)
        up = conv(d_ff + c * tf)
        act[:, c * tf:(c + 1) * tf] = (_gelu_tanh(gate) * up).astype(BF16)
    y = jnp.dot(act[...], wd_ref[...], preferred_element_type=F32)
    o_ref[...] = x + _rms(y, gpost_ref[...])


def _ffn(l, x, gpre, wup, cw, cb, wd, gpost):
    B, S, D = x.shape
    d_ff = wd.shape[1]
    tm = FFN_ROW_TILE
    tf = MXU_DIM
    row = pl.BlockSpec((None, tm, D), lambda b, s: (b, s, 0))
    return pl.pallas_call(
        functools.partial(_ffn_body, d_ff=d_ff, tf=tf),
        grid=(B, S // tm),
        in_specs=[row, _layer(gpre, l), _layer(wup, 0), _layer(cw, l), _layer(cb, l),
                  _layer(wd, 0), _layer(gpost, l)],
        out_specs=row,
        out_shape=jax.ShapeDtypeStruct((B, S, D), F32),
        scratch_shapes=[pltpu.VMEM((SUBLANES, 2 * d_ff), F32),
                        pltpu.VMEM((tm, d_ff), BF16)],
        compiler_params=_params("arbitrary", "arbitrary"),
        name="conv_ffn",
    )(x, gpre, wup, cw, cb, wd, gpost)


def _rope_tables(seq, heads):
    inv = 1.0 / (ROPE_THETA ** (jnp.arange(0, HEAD_DIM, 2, dtype=F32) / HEAD_DIM))
    ang = jnp.arange(seq, dtype=F32)[:, None] * inv[None, :]
    cos, sin = jnp.cos(ang), jnp.sin(ang)
    cos_h = jnp.concatenate([cos, cos], axis=-1)
    sin_h = jnp.concatenate([-sin, sin], axis=-1)
    return jnp.tile(cos_h, (1, heads)), jnp.tile(sin_h, (1, heads))


def _block_diag(w):
    depth, h, n, _ = w.shape
    eye = jnp.eye(h, dtype=w.dtype)
    return jnp.einsum('lhij,hg->lhigj', w, eye).reshape(depth, h * n, h * n)


def kernel(x, mem, g_mix_pre, w_in, conv_w, conv_b, conv_ln_g, conv_ln_b, lru_conv_w, lru_conv_b, lru_w_a, lru_b_a, lru_w_i, lru_b_i, lru_lambda, w_out, g_mix_post, g_mem_pre, g_mem_kv, w_mem_q, w_mem_kv, w_mem_o, g_mem_post, g_ffn_pre, w_up, ffn_conv_w, ffn_conv_b, w_down, g_ffn_post):
    B, S, D = x.shape
    depth = w_in.shape[0]
    conv_ch = conv_w.shape[-1]
    lru_w = lru_conv_w.shape[-1]
    att_w = w_out.shape[1] - conv_ch - lru_w
    n_mem = mem.shape[1]
    assert S % ROW_TILE == 0 and ROW_TILE % (ATT_PERM * SUBLANES * 2) == 0
    assert att_w % LANES == 0 and LANES == 2 * HEAD_DIM and (B * n_mem) % ROW_TILE == 0

    cos, sin = _rope_tables(S, att_w // HEAD_DIM)
    near, far = _attn_bias_tables(S)
    vec = lambda p: p.reshape(depth, 1, -1)
    mxu = lambda p: p.astype(BF16)
    mem2 = mem.reshape(B * n_mem, D)

    g_mix_pre, g_mix_post, g_mem_pre, g_mem_kv, g_mem_post, g_ffn_pre, g_ffn_post = map(
        vec, (g_mix_pre, g_mix_post, g_mem_pre, g_mem_kv, g_mem_post, g_ffn_pre, g_ffn_post))
    conv_b, conv_ln_g, conv_ln_b, lru_conv_b, lru_b_a, lru_b_i, lru_lambda, ffn_conv_b = map(
        vec, (conv_b, conv_ln_g, conv_ln_b, lru_conv_b, lru_b_a, lru_b_i, lru_lambda, ffn_conv_b))
    lru_w_a = mxu(_block_diag(lru_w_a))
    lru_w_i = mxu(_block_diag(lru_w_i))
    kv = _mem_kv_proj(mem2, g_mem_kv, w_mem_kv).reshape(depth, B, n_mem, 2 * D)

    for l in range(depth):
        y_a, b_xg, qkv, qkvp, w_up_l, w_down_l = _in_proj(
            l, x, g_mix_pre, w_in, cos, sin, conv_w, conv_b, conv_ln_g, conv_ln_b, w_up, w_down,
            conv_ch, lru_w, att_w)
        y_b = _lru_mixer(l, b_xg, lru_conv_w, lru_conv_b, lru_w_a, lru_b_a, lru_w_i, lru_b_i,
                         lru_lambda)
        y_c = _dilated_attention(qkv, qkvp, near, far)
        x = _mix_out_mem(l, y_a, y_b, y_c, x, w_out, g_mix_post, g_mem_pre, w_mem_q, kv, w_mem_o,
                         g_mem_post)
        x = _ffn(l, x, g_ffn_pre, w_up_l, ffn_conv_w, ffn_conv_b, w_down_l, g_ffn_post)
    return x
```

```python
import functools
import math

import numpy as np
import jax
import jax.numpy as jnp
from jax import lax
from jax.experimental import pallas as pl
from jax.experimental.pallas import tpu as pltpu

F32 = jnp.float32
BF16 = jnp.bfloat16

HEAD_DIM = 64
CONV_KERNEL = 31
LRU_CONV_KERNEL = 4
LRU_C = 8.0
DILATED_PATTERNS = ((128, 1), (512, 4), (2048, 16))
ROPE_THETA = 10000.0
MEM_HEADS = 4
FFN_CONV_KERNEL = 3
EPS = 1e-6

LANES = 128
SUBLANES = 8
MXU_DIM = 256
VMEM_LIMIT_BYTES = 56 * 1024 * 1024

ROW_TILE = 512
FFN_ROW_TILE = 1024
MIX_ROW_TILE = 1024
ATT_BLOCK = 128
ATT_PERM = 4
ATT_PAIRS_PER_STEP = 2
ATT_PIPELINE = 4
CONV_HALO = 32
CONV_SUB = 64
MIX_ROW_GROUPS = 4
LOG2E = 1.4426950408889634
LRU_BATCH_PER_STEP = 2
SCAN_CHUNKS = SUBLANES
NEG_BIAS = -1e30


def _params(*sem):
    return pltpu.CompilerParams(dimension_semantics=sem, vmem_limit_bytes=VMEM_LIMIT_BYTES)


def _rms(x, g):
    return x * lax.rsqrt(jnp.mean(x * x, axis=-1, keepdims=True) + EPS) * g


def _sigmoid(x):
    return 1.0 / (1.0 + jnp.exp2(x * (-LOG2E)))


def _gelu_tanh(x):
    k = -2.0 * math.sqrt(2.0 / math.pi) * LOG2E
    return x / (1.0 + jnp.exp2(x * (k + (0.044715 * k) * (x * x))))


def _full(shape):
    n = len(shape)
    return pl.BlockSpec(shape, lambda *_: (0,) * n)


def _layer(p, l):
    return pl.BlockSpec((None,) + p.shape[1:], lambda *_: (l, 0, 0), pipeline_mode=pl.Buffered(1))


def _cast_weights_once(grid_rank, pairs):
    first = pl.program_id(0) == 0
    for axis in range(1, grid_rank):
        first = jnp.logical_and(first, pl.program_id(axis) == 0)

    @pl.when(first)
    def _():
        for src, dst in pairs:
            dst[...] = src[...].astype(dst.dtype)


def _mxu_copy(w):
    return pltpu.VMEM(w.shape[1:], BF16)


def _rope(t, cos, sin_signed, first_half):
    n = t.shape[-1]
    half = HEAD_DIM // 2
    partner = jnp.where(first_half, pltpu.roll(t, n - half, axis=1), pltpu.roll(t, half, axis=1))
    return t * cos + partner * sin_signed


def _store_attn_operand(t, nat_ref, perm_ref, stage):
    rows, width = t.shape
    nat_ref[...] = t.astype(BF16)
    for c in range(width // LANES):
        stage[c] = t[:, c * LANES:(c + 1) * LANES]
    for c in range(width // LANES):
        for r in range(ATT_PERM):
            sub = stage[c, pl.ds(r, rows // ATT_PERM, stride=ATT_PERM), :]
            perm_ref[r, :, c * LANES:(c + 1) * LANES] = sub.astype(BF16)


def _conv_mixer_halo(ubuf, tm):
    @pl.when(pl.program_id(1) == 0)
    def _():
        ubuf[0:CONV_HALO, :] = jnp.zeros((CONV_HALO, ubuf.shape[-1]), F32)

    @pl.when(pl.program_id(1) > 0)
    def _():
        ubuf[0:CONV_HALO, :] = ubuf[tm:tm + CONV_HALO, :]


def _conv_mixer_prepare(a, ubuf, ush, *, ch):
    tm = a.shape[0]
    ubuf[CONV_HALO:CONV_HALO + tm, :] = a[:, :ch] * _sigmoid(a[:, ch:])
    span = CONV_HALO + tm - SUBLANES
    for s in range(1, SUBLANES):
        ush[s - 1, 0:span, :] = ubuf[s:s + span, :]


def _conv_mixer_rows(r0, w_ref, b_ref, g_ref, beta_ref, o_ref, ubuf, ush):
    ch = ubuf.shape[-1]
    w = w_ref[...]
    off = CONV_HALO - (CONV_KERNEL - 1)
    acc = jnp.zeros((CONV_SUB, ch), F32) + b_ref[...]
    for j in range(CONV_KERNEL):
        s = (off + j) % SUBLANES
        lo = r0 + off + j - s
        win = ubuf[lo:lo + CONV_SUB, :] if s == 0 else ush[s - 1, lo:lo + CONV_SUB, :]
        acc = acc + w[j:j + 1, :] * win
    mu = jnp.mean(acc, axis=-1, keepdims=True)
    cen = acc - mu
    var = jnp.mean(cen * cen, axis=-1, keepdims=True)
    y = cen * lax.rsqrt(var + EPS) * g_ref[...] + beta_ref[...]
    o_ref[r0:r0 + CONV_SUB, :] = (y * _sigmoid(y)).astype(o_ref.dtype)


def _in_proj_body(x_ref, g_ref, w32_ref, cos_ref, sin_ref, cw_ref, cb_ref, cg_ref, cbeta_ref,
                  wup32_ref, wdown32_ref,
                  ya_ref, lru_ref, qkv_ref, qkvp_ref, wup_ref, wdown_ref,
                  w_ref, ubuf, ush, stage_q, stage_k, stage_v, *, splits, n_down_pieces):
    tm = x_ref.shape[0]
    _cast_weights_once(2, [(w32_ref, w_ref)])
    step = pl.program_id(0) * pl.num_programs(1) + pl.program_id(1)
    wup_ref[...] = wup32_ref[...].astype(BF16)

    @pl.when(step < n_down_pieces)
    def _():
        wdown_ref[...] = wdown32_ref[...].astype(BF16)

    _conv_mixer_halo(ubuf, tm)
    hb = _rms(x_ref[...], g_ref[...]).astype(BF16)

    def proj(lo, hi):
        return jnp.dot(hb, w_ref[:, lo:hi], preferred_element_type=F32)

    c_a, c_bx, c_bg, c_q, c_k, c_v = splits
    rows = pl.ds(pl.multiple_of(pl.program_id(1) * tm, tm), tm)
    cos = cos_ref[rows, :]
    sin = sin_ref[rows, :]
    lane = lax.broadcasted_iota(jnp.int32, cos.shape, 1)
    first_half = (lane % HEAD_DIM) < (HEAD_DIM // 2)
    _conv_mixer_prepare(proj(0, c_a), ubuf, ush, ch=c_a // 2)
    for r0 in range(0, tm, CONV_SUB):
        _conv_mixer_rows(r0, cw_ref, cb_ref, cg_ref, cbeta_ref, ya_ref, ubuf, ush)
    lru_ref[...] = proj(c_a, c_bg)
    att_w = c_q - c_bg
    cols = [slice(i * att_w, (i + 1) * att_w) for i in range(3)]
    q = _rope(proj(c_bg, c_q), cos, sin, first_half) * (HEAD_DIM ** -0.5 * LOG2E)
    _store_attn_operand(q, qkv_ref.at[:, cols[0]], qkvp_ref.at[:, :, cols[0]], stage_q)
    _store_attn_operand(_rope(proj(c_q, c_k), cos, sin, first_half),
                        qkv_ref.at[:, cols[1]], qkvp_ref.at[:, :, cols[1]], stage_k)
    _store_attn_operand(proj(c_k, c_v), qkv_ref.at[:, cols[2]], qkvp_ref.at[:, :, cols[2]], stage_v)


def _in_proj(l, x, g, w, cos, sin, conv_w, conv_b, conv_g, conv_beta, w_up, w_down,
             conv_ch, lru_w, att_w):
    B, S, D = x.shape
    tm = ROW_TILE
    n_steps = B * (S // tm)
    bf16_rows = 2 * SUBLANES
    up_rows = w_up.shape[1] // n_steps
    down_rows = LANES
    n_down = w_down.shape[1] // down_rows
    assert up_rows * n_steps == w_up.shape[1] and up_rows % bf16_rows == 0
    assert n_down * down_rows == w_down.shape[1] and n_down <= n_steps
    steps_per_b = S // tm
    up_piece = lambda b, s: (l, b * steps_per_b + s, 0)
    down_piece = lambda b, s: (l, jnp.minimum(b * steps_per_b + s, n_down - 1), 0)
    splits = (2 * conv_ch, 2 * conv_ch + lru_w, 2 * conv_ch + 2 * lru_w,
              2 * conv_ch + 2 * lru_w + att_w, 2 * conv_ch + 2 * lru_w + 2 * att_w,
              2 * conv_ch + 2 * lru_w + 3 * att_w)
    row = lambda n: pl.BlockSpec((None, tm, n), lambda b, s: (b, s, 0))
    perm = pl.BlockSpec((None, ATT_PERM, tm // ATT_PERM, 3 * att_w), lambda b, s: (b, 0, s, 0))
    stage = pltpu.VMEM((att_w // LANES, tm, LANES), F32)
    return pl.pallas_call(
        functools.partial(_in_proj_body, splits=splits, n_down_pieces=n_down),
        grid=(B, S // tm),
        in_specs=[row(D), _layer(g, l), _layer(w, l), _full(cos.shape), _full(sin.shape),
                  _layer(conv_w, l), _layer(conv_b, l), _layer(conv_g, l), _layer(conv_beta, l),
                  pl.BlockSpec((None, up_rows, w_up.shape[2]), up_piece),
                  pl.BlockSpec((None, down_rows, w_down.shape[2]), down_piece)],
        out_specs=[row(conv_ch), row(2 * lru_w), row(3 * att_w), perm,
                   pl.BlockSpec((None, up_rows, w_up.shape[2]), lambda b, s: (0,) + up_piece(b, s)[1:]),
                   pl.BlockSpec((None, down_rows, w_down.shape[2]),
                                lambda b, s: (0,) + down_piece(b, s)[1:])],
        out_shape=[jax.ShapeDtypeStruct((B, S, conv_ch), BF16),
                   jax.ShapeDtypeStruct((B, S, 2 * lru_w), F32),
                   jax.ShapeDtypeStruct((B, S, 3 * att_w), BF16),
                   jax.ShapeDtypeStruct((B, ATT_PERM, S // ATT_PERM, 3 * att_w), BF16),
                   jax.ShapeDtypeStruct((1,) + w_up.shape[1:], BF16),
                   jax.ShapeDtypeStruct((1,) + w_down.shape[1:], BF16)],
        scratch_shapes=[_mxu_copy(w),
                        pltpu.VMEM((CONV_HALO + tm, conv_ch), F32),
                        pltpu.VMEM((SUBLANES - 1, CONV_HALO + tm - SUBLANES, conv_ch), F32),
                        stage, stage, stage],
        compiler_params=_params("arbitrary", "arbitrary"),
        name="in_proj",
    )(x, g, w, cos, sin, conv_w, conv_b, conv_g, conv_beta, w_up, w_down)


def _lru_body(bx_ref, bg_ref, cw_ref, cb_ref, wa_ref, ba_ref, wi_ref, bi_ref, lam_ref, o_ref,
              a_s, b_s, *, pitch):
    nb, S, C = bx_ref.shape
    n_slab = C // LANES
    chunk = S // SCAN_CHUNKS
    pad = SUBLANES
    first_rows = lax.broadcasted_iota(jnp.int32, (pad, C), 0)
    z = -lam_ref[...]
    log_a_scale = (-LRU_C) * (jnp.maximum(z, 0.0) + jnp.log1p(jnp.exp(-jnp.abs(z))))
    cw = cw_ref[...]
    for bi, j in [(bi, j) for bi in range(nb) for j in range(SCAN_CHUNKS)]:
        base = j * chunk
        xc = bx_ref[bi, base:base + chunk, :]
        last = bx_ref[bi, base - pad:base, :] if j > 0 else jnp.zeros((pad, C), F32)
        u = cb_ref[...] + cw[LRU_CONV_KERNEL - 1:LRU_CONV_KERNEL, :] * xc
        for t in range(LRU_CONV_KERNEL - 1):
            back = LRU_CONV_KERNEL - 1 - t
            shifted = pltpu.roll(xc, back, axis=0)
            head = jnp.where(first_rows < back, pltpu.roll(last, back, axis=0), shifted[0:pad, :])
            u = u + cw[t:t + 1, :] * jnp.concatenate([head, shifted[pad:, :]], axis=0)
        ub = u.astype(BF16)
        r = _sigmoid(jnp.dot(ub, wa_ref[...], preferred_element_type=F32) + ba_ref[...])
        gate_i = _sigmoid(jnp.dot(ub, wi_ref[...], preferred_element_type=F32) + bi_ref[...])
        log_a = r * log_a_scale
        a = jnp.exp(log_a)
        bterm = jnp.sqrt(jnp.tanh(-log_a) * (1.0 + a * a)) * (gate_i * u)
        for s in range(n_slab):
            a_s[bi * n_slab + s, j * pitch:j * pitch + chunk, :] = a[:, s * LANES:(s + 1) * LANES]
            b_s[bi * n_slab + s, j * pitch:j * pitch + chunk, :] = bterm[:, s * LANES:(s + 1) * LANES]

    def step(i, carry):
        out = []
        for s in range(nb * n_slab):
            h, p = carry[2 * s], carry[2 * s + 1]
            idx = pl.ds(i, SCAN_CHUNKS, stride=pitch)
            av = a_s[s, idx, :]
            h = av * h + b_s[s, idx, :]
            p = av * p
            b_s[s, idx, :] = h
            a_s[s, idx, :] = p
            out += [h, p]
        return tuple(out)

    init = []
    for s in range(nb * n_slab):
        init += [jnp.zeros((SCAN_CHUNKS, LANES), F32), jnp.ones((SCAN_CHUNKS, LANES), F32)]
    fin = lax.fori_loop(0, chunk, step, tuple(init), unroll=8)

    for bi, s in [(bi, s) for bi in range(nb) for s in range(n_slab)]:
        k = bi * n_slab + s
        h_end, p_end = fin[2 * k], fin[2 * k + 1]
        carry_in = jnp.zeros((1, LANES), F32)
        for j in range(SCAN_CHUNKS):
            rows = slice(j * pitch, j * pitch + chunk)
            h = b_s[k, rows, :] + a_s[k, rows, :] * carry_in
            gate = _gelu_tanh(bg_ref[bi, j * chunk:(j + 1) * chunk, s * LANES:(s + 1) * LANES])
            o_ref[bi, j * chunk:(j + 1) * chunk, s * LANES:(s + 1) * LANES] = (h * gate).astype(o_ref.dtype)
            carry_in = h_end[j:j + 1, :] + p_end[j:j + 1, :] * carry_in


def _lru_mixer(l, b_xg, cw, cb, wa, ba, wi, bi, lam):
    B, S, C = b_xg.shape[0], b_xg.shape[1], b_xg.shape[2] // 2
    chunk = S // SCAN_CHUNKS
    pitch = chunk + SUBLANES
    nb = LRU_BATCH_PER_STEP
    assert B % nb == 0
    seq = pl.BlockSpec((nb, S, C), lambda b: (b, 0, 0))
    gate = pl.BlockSpec((nb, S, C), lambda b: (b, 0, 1))
    params = (cw, cb, wa, ba, wi, bi, lam)
    scan = pltpu.VMEM((nb * (C // LANES), SCAN_CHUNKS * pitch, LANES), F32)
    return pl.pallas_call(
        functools.partial(_lru_body, pitch=pitch),
        grid=(B // nb,),
        in_specs=[seq, gate] + [_layer(p, l) for p in params],
        out_specs=seq,
        out_shape=jax.ShapeDtypeStruct((B, S, C), BF16),
        scratch_shapes=[scan, scan],
        compiler_params=_params("arbitrary"),
        name="lru_mixer",
    )(b_xg, b_xg, *params)


def _attn_bias_tables(seq):
    blk = ATT_BLOCK
    local = [(w, d) for w, d in DILATED_PATTERNS if d == 1]
    strided = [(w, d) for w, d in DILATED_PATTERNS if d != 1]
    for window, dil in DILATED_PATTERNS:
        n = window // dil
        assert seq % dil == 0 and (seq // dil) % n == 0, "pattern padding not supported"
    assert all(w <= blk for w, _ in local) and all(d % ATT_PERM == 0 for _, d in strided)
    assert seq % (ATT_PERM * blk) == 0

    def bias(mult):
        return jnp.asarray(np.where(mult > 0, np.log2(np.maximum(mult, 1)), NEG_BIAS), F32)

    r = np.arange(blk)[:, None]
    d = r - np.arange(2 * blk)[None, :] + blk
    near = sum(((d >= 0) & (d <= w)).astype(np.int64) for w, _ in local)
    n_sub = seq // ATT_PERM // blk
    d = ATT_PERM * (r - np.arange(n_sub * blk)[None, :] + (n_sub - 1) * blk)
    far = sum(((d >= 0) & (d % dil == 0) & (d <= w)).astype(np.int64) for w, dil in strided)
    return bias(near), bias(far)


def _scores(q, k):
    return lax.dot_general(q, k, (((1,), (1,)), ((), ())), preferred_element_type=F32)


def _attn_head_pair(q_ref, k_ref, v_ref, qp_ref, kp_ref, vp_ref, near_ref, far_ref, o_ref,
                    vn, vp, acc_far, max_far):
    S = q_ref.shape[0]
    blk = ATT_BLOCK
    n_res, sub_len = qp_ref.shape[0], qp_ref.shape[1]
    n_sub = sub_len // blk
    head0 = lax.broadcasted_iota(jnp.int32, (blk, LANES), 1) < HEAD_DIM
    head0_sub = lax.broadcasted_iota(jnp.int32, (sub_len, LANES), 1) < HEAD_DIM
    one = jnp.ones((sub_len, LANES), BF16)
    for h in range(2):
        own = head0_sub if h == 0 else jnp.logical_not(head0_sub)
        for r in range(n_res):
            vp[h, r] = jnp.where(own, vp_ref[r], one)
        for c in range(S // sub_len):
            rows = slice(c * sub_len, (c + 1) * sub_len)
            vn[h, rows, :] = jnp.where(own, v_ref[rows, :], one)

    def own_lanes(h):
        return head0 if h == 0 else jnp.logical_not(head0)

    def far_block(h, r, a):
        nk = (a + 1) * blk

        def scores():
            q = qp_ref[r, a * blk:(a + 1) * blk, :]
            q = jnp.where(own_lanes(h), q, jnp.zeros_like(q))
            return _scores(q, kp_ref[r, 0:nk, :]) + far_ref[:, (n_sub - 1 - a) * blk:]

        def finish(s):
            m = jnp.max(s, axis=-1, keepdims=True)
            p = jnp.exp2(s - m).astype(BF16)
            rows = pl.ds(r + n_res * a * blk, blk, stride=n_res)
            acc_far[h, rows, :] = jnp.dot(p, vp[h, r, 0:nk, :], preferred_element_type=F32)
            max_far[h, rows, :] = jnp.broadcast_to(m, (blk, LANES))

        return scores, finish

    def near_block(i):
        rows = slice(i * blk, (i + 1) * blk)
        keys = slice(max(i - 1, 0) * blk, (i + 1) * blk)
        bias = near_ref[...] if i > 0 else near_ref[:, blk:]

        def scores():
            q = q_ref[rows, :]
            return [_scores(jnp.where(own_lanes(h), q, jnp.zeros_like(q)), k_ref[keys, :]) + bias
                    for h in range(2)]

        def finish(s_heads):
            acc = []
            for h, s in enumerate(s_heads):
                m_far = max_far[h, rows, :]
                m = jnp.maximum(jnp.max(s, axis=-1, keepdims=True), m_far)
                m_keys = m if i == 0 else jnp.concatenate([m, m], axis=-1)
                p = jnp.exp2(s - m_keys).astype(BF16)
                acc.append(jnp.dot(p, vn[h, keys, :], preferred_element_type=F32)
                           + jnp.exp2(m_far - m) * acc_far[h, rows, :])
            num = jnp.where(head0, acc[0], acc[1])
            den = pltpu.roll(jnp.where(head0, acc[1], acc[0]), HEAD_DIM, axis=1)
            o_ref[rows, :] = (num / den).astype(o_ref.dtype)

        return scores, finish

    blocks = []
    for a in range(n_sub):
        blocks += [far_block(h, r, a) for r in range(n_res) for h in range(2)]
        blocks += [near_block(i) for i in range(a * n_res, (a + 1) * n_res)]
    return blocks


def _attn_body(q_ref, k_ref, v_ref, qp_ref, kp_ref, vp_ref, near_ref, far_ref, o_ref,
               vn, vp, acc_far, max_far):
    blocks = []
    for p in range(q_ref.shape[-1] // LANES):
        lanes = slice(p * LANES, (p + 1) * LANES)
        blocks += _attn_head_pair(
            q_ref.at[:, lanes], k_ref.at[:, lanes], v_ref.at[:, lanes],
            qp_ref.at[:, :, lanes], kp_ref.at[:, :, lanes], vp_ref.at[:, :, lanes],
            near_ref, far_ref, o_ref.at[:, lanes],
            vn.at[p], vp.at[p], acc_far.at[p], max_far.at[p])
    pending = []
    for scores, finish in blocks:
        pending.append((finish, scores()))
        if len(pending) > ATT_PIPELINE:
            fn, s = pending.pop(0)
            fn(s)
    for fn, s in pending:
        fn(s)


def _dilated_attention(qkv, qkvp, near, far):
    B, S, W = qkv.shape[0], qkv.shape[1], qkv.shape[2] // 3
    n_res, sub_len = qkvp.shape[1], qkvp.shape[2]
    pairs = ATT_PAIRS_PER_STEP
    width = pairs * LANES
    steps = W // width
    nat = lambda part: pl.BlockSpec((None, S, width), lambda b, h: (b, 0, part * steps + h))
    perm = lambda part: pl.BlockSpec((None, n_res, sub_len, width),
                                     lambda b, h: (b, 0, 0, part * steps + h))
    return pl.pallas_call(
        _attn_body,
        grid=(B, steps),
        in_specs=[nat(0), nat(1), nat(2), perm(0), perm(1), perm(2),
                  _full(near.shape), _full(far.shape)],
        out_specs=nat(0),
        out_shape=jax.ShapeDtypeStruct((B, S, W), BF16),
        scratch_shapes=[pltpu.VMEM((pairs, 2, S, LANES), BF16),
                        pltpu.VMEM((pairs, 2, n_res, sub_len, LANES), BF16),
                        pltpu.VMEM((pairs, 2, S, LANES), F32),
                        pltpu.VMEM((pairs, 2, S, LANES), F32)],
        compiler_params=_params("arbitrary", "arbitrary"),
        name="dilated_attn",
    )(qkv, qkv, qkv, qkvp, qkvp, qkvp, near, far)


def _norm_matmul_body(x_ref, g_ref, w32_ref, o_ref, w_ref):
    @pl.when(pl.program_id(1) == 0)
    def _():
        w_ref[...] = w32_ref[...].astype(BF16)

    hb = _rms(x_ref[...], g_ref[...]).astype(BF16)
    o_ref[...] = jnp.dot(hb, w_ref[...], preferred_element_type=F32).astype(o_ref.dtype)


def _mem_kv_proj(x, g, w):
    T, D = x.shape
    depth, _, N = w.shape
    tm = ROW_TILE
    per_layer = lambda p: pl.BlockSpec((None,) + p.shape[1:], lambda l, i: (l, 0, 0))
    return pl.pallas_call(
        _norm_matmul_body,
        grid=(depth, T // tm),
        in_specs=[pl.BlockSpec((tm, D), lambda l, i: (i, 0)), per_layer(g), per_layer(w)],
        out_specs=pl.BlockSpec((None, tm, N), lambda l, i: (l, i, 0)),
        out_shape=jax.ShapeDtypeStruct((depth, T, N), BF16),
        scratch_shapes=[_mxu_copy(w)],
        compiler_params=_params("arbitrary", "arbitrary"),
        name="mem_kv_proj",
    )(x, g, w)


def _mix_out_mem_body(ya_ref, yb_ref, yc_ref, x_ref, wout32_ref, gmix_ref, gpre_ref, wq32_ref, kv_ref,
                      wo32_ref, gpost_ref, o_ref, wout_ref, wq_ref, wo_ref):
    _cast_weights_once(2, [(wout32_ref, wout_ref), (wq32_ref, wq_ref), (wo32_ref, wo_ref)])
    tm, D = x_ref.shape
    hd = D // MEM_HEADS

    def stages(rows):
        y = jnp.concatenate([ya_ref[rows, :], yb_ref[rows, :], yc_ref[rows, :]], axis=-1)
        y = jnp.dot(y, wout_ref[...], preferred_element_type=F32)
        yield
        x = x_ref[rows, :] + _rms(y, gmix_ref[...])
        hb = _rms(x, gpre_ref[...]).astype(BF16)
        yield
        q = jnp.dot(hb, wq_ref[...], preferred_element_type=F32) * (hd ** -0.5 * LOG2E)
        q = q.astype(BF16)
        scores = [_scores(q[:, h * hd:(h + 1) * hd], kv_ref[:, h * hd:(h + 1) * hd])
                  for h in range(MEM_HEADS)]
        yield
        probs = []
        for s in scores:
            p = jnp.exp2(s - jnp.max(s, axis=-1, keepdims=True))
            probs.append((p.astype(BF16), jnp.sum(p, axis=-1, keepdims=True)))
        yield
        outs = []
        for h, (p, den) in enumerate(probs):
            vh = kv_ref[:, D + h * hd:D + (h + 1) * hd]
            outs.append((jnp.dot(p, vh, preferred_element_type=F32) / den).astype(BF16))
        y = jnp.dot(jnp.concatenate(outs, axis=-1), wo_ref[...], preferred_element_type=F32)
        yield
        o_ref[rows, :] = x + _rms(y, gpost_ref[...])

    rows_per_group = tm // MIX_ROW_GROUPS
    chains = [stages(slice(g * rows_per_group, (g + 1) * rows_per_group))
              for g in range(MIX_ROW_GROUPS)]
    while chains:
        chains = [c for c in chains if next(c, StopIteration) is not StopIteration]


def _mix_out_mem(l, ya, yb, yc, x, wout, gmix, gpre, wq, kv, wo, gpost):
    B, S, D = x.shape
    M = kv.shape[2]
    tm = MIX_ROW_TILE
    row = lambda n: pl.BlockSpec((None, tm, n), lambda b, s: (b, s, 0))
    return pl.pallas_call(
        _mix_out_mem_body,
        grid=(B, S // tm),
        in_specs=[row(ya.shape[-1]), row(yb.shape[-1]), row(yc.shape[-1]), row(D),
                  _layer(wout, l), _layer(gmix, l), _layer(gpre, l), _layer(wq, l),
                  pl.BlockSpec((None, None, M, 2 * D), lambda b, s: (l, b, 0, 0)),
                  _layer(wo, l), _layer(gpost, l)],
        out_specs=row(D),
        out_shape=jax.ShapeDtypeStruct((B, S, D), F32),
        scratch_shapes=[_mxu_copy(wout), _mxu_copy(wq), _mxu_copy(wo)],
        compiler_params=_params("arbitrary", "arbitrary"),
        name="mix_out_mem_attn",
    )(ya, yb, yc, x, wout, gmix, gpre, wq, kv, wo, gpost)


def _ffn_body(x_ref, gpre_ref, wup_ref, cw_ref, cb_ref, wd_ref, gpost_ref, o_ref,
              carry, act, *, d_ff, tf):
    tm = x_ref.shape[0]
    pad = SUBLANES
    n_chunks = d_ff // tf
    first_rows = lax.broadcasted_iota(jnp.int32, (pad, tf), 0)

    @pl.when(pl.program_id(1) == 0)
    def _():
        carry[...] = jnp.zeros(carry.shape, F32)

    x = x_ref[...]
    hb = _rms(x, gpre_ref[...]).astype(BF16)

    def conv(col0):
        cols = slice(col0, col0 + tf)
        u = jnp.dot(hb, wup_ref[:, cols], preferred_element_type=F32)
        last = carry[:, cols]
        carry[:, cols] = u[tm - pad:tm, :]
        w = cw_ref[:, cols]
        y = cb_ref[:, cols] + w[FFN_CONV_KERNEL - 1:FFN_CONV_KERNEL, :] * u
        for t in range(FFN_CONV_KERNEL - 1):
            back = FFN_CONV_KERNEL - 1 - t
            shifted = pltpu.roll(u, back, axis=0)
            head = jnp.where(first_rows < back, pltpu.roll(last, back, axis=0), shifted[0:pad, :])
            y = y + w[t:t + 1, :] * jnp.concatenate([head, shifted[pad:, :]], axis=0)
        return y

    for c in range(n_chunks):
        gate = conv(c * tf)
        up = conv(d_ff + c * tf)
        act[:, c * tf:(c + 1) * tf] = (_gelu_tanh(gate) * up).astype(BF16)
    y = jnp.dot(act[...], wd_ref[...], preferred_element_type=F32)
    o_ref[...] = x + _rms(y, gpost_ref[...])


def _ffn(l, x, gpre, wup, cw, cb, wd, gpost):
    B, S, D = x.shape
    d_ff = wd.shape[1]
    tm = FFN_ROW_TILE
    tf = MXU_DIM
    row = pl.BlockSpec((None, tm, D), lambda b, s: (b, s, 0))
    return pl.pallas_call(
        functools.partial(_ffn_body, d_ff=d_ff, tf=tf),
        grid=(B, S // tm),
        in_specs=[row, _layer(gpre, l), _layer(wup, 0), _layer(cw, l), _layer(cb, l),
                  _layer(wd, 0), _layer(gpost, l)],
        out_specs=row,
        out_shape=jax.ShapeDtypeStruct((B, S, D), F32),
        scratch_shapes=[pltpu.VMEM((SUBLANES, 2 * d_ff), F32),
                        pltpu.VMEM((tm, d_ff), BF16)],
        compiler_params=_params("arbitrary", "arbitrary"),
        name="conv_ffn",
    )(x, gpre, wup, cw, cb, wd, gpost)


def _rope_tables(seq, heads):
    inv = 1.0 / (ROPE_THETA ** (jnp.arange(0, HEAD_DIM, 2, dtype=F32) / HEAD_DIM))
    ang = jnp.arange(seq, dtype=F32)[:, None] * inv[None, :]
    cos, sin = jnp.cos(ang), jnp.sin(ang)
    cos_h = jnp.concatenate([cos, cos], axis=-1)
    sin_h = jnp.concatenate([-sin, sin], axis=-1)
    return jnp.tile(cos_h, (1, heads)), jnp.tile(sin_h, (1, heads))


def _block_diag(w):
    depth, h, n, _ = w.shape
    eye = jnp.eye(h, dtype=w.dtype)
    return jnp.einsum('lhij,hg->lhigj', w, eye).reshape(depth, h * n, h * n)


def kernel(x, mem, g_mix_pre, w_in, conv_w, conv_b, conv_ln_g, conv_ln_b, lru_conv_w, lru_conv_b, lru_w_a, lru_b_a, lru_w_i, lru_b_i, lru_lambda, w_out, g_mix_post, g_mem_pre, g_mem_kv, w_mem_q, w_mem_kv, w_mem_o, g_mem_post, g_ffn_pre, w_up, ffn_conv_w, ffn_conv_b, w_down, g_ffn_post):
    B, S, D = x.shape
    depth = w_in.shape[0]
    conv_ch = conv_w.shape[-1]
    lru_w = lru_conv_w.shape[-1]
    att_w = w_out.shape[1] - conv_ch - lru_w
    n_mem = mem.shape[1]
    assert S % ROW_TILE == 0 and ROW_TILE % (ATT_PERM * SUBLANES * 2) == 0
    assert att_w % LANES == 0 and LANES == 2 * HEAD_DIM and (B * n_mem) % ROW_TILE == 0

    cos, sin = _rope_tables(S, att_w // HEAD_DIM)
    near, far = _attn_bias_tables(S)
    vec = lambda p: p.reshape(depth, 1, -1)
    mxu = lambda p: p.astype(BF16)
    mem2 = mem.reshape(B * n_mem, D)

    g_mix_pre, g_mix_post, g_mem_pre, g_mem_kv, g_mem_post, g_ffn_pre, g_ffn_post = map(
        vec, (g_mix_pre, g_mix_post, g_mem_pre, g_mem_kv, g_mem_post, g_ffn_pre, g_ffn_post))
    conv_b, conv_ln_g, conv_ln_b, lru_conv_b, lru_b_a, lru_b_i, lru_lambda, ffn_conv_b = map(
        vec, (conv_b, conv_ln_g, conv_ln_b, lru_conv_b, lru_b_a, lru_b_i, lru_lambda, ffn_conv_b))
    lru_w_a = mxu(_block_diag(lru_w_a))
    lru_w_i = mxu(_block_diag(lru_w_i))
    kv = _mem_kv_proj(mem2, g_mem_kv, w_mem_kv).reshape(depth, B, n_mem, 2 * D)

    for l in range(depth):
        y_a, b_xg, qkv, qkvp, w_up_l, w_down_l = _in_proj(
            l, x, g_mix_pre, w_in, cos, sin, conv_w, conv_b, conv_ln_g, conv_ln_b, w_up, w_down,
            conv_ch, lru_w, att_w)
        y_b = _lru_mixer(l, b_xg, lru_conv_w, lru_conv_b, lru_w_a, lru_b_a, lru_w_i, lru_b_i,
                         lru_lambda)
        y_c = _dilated_attention(qkv, qkvp, near, far)
        x = _mix_out_mem(l, y_a, y_b, y_c, x, w_out, g_mix_post, g_mem_pre, w_mem_q, kv, w_mem_o,
                         g_mem_post)
        x = _ffn(l, x, g_ffn_pre, w_up_l, ffn_conv_w, ffn_conv_b, w_down_l, g_ffn_post)
    return x
```
